```python
import math
import jax, jax.numpy as jnp
from jax import lax
import numpy as np

D_MODEL = 2048
BATCH = 1
SEQ = 16384
DEPTH = 1

D_MIX = D_MODEL
ATTN_WIDTH = D_MIX // 2
CONV_WIDTH = D_MIX - ATTN_WIDTH
HEAD_DIM = 128
N_ATTN_HEADS = ATTN_WIDTH // HEAD_DIM
CONV_KERNEL = 31
Q_BLOCK = 128
RMS_EPS = 1e-6
LN_EPS = 1e-5
IN_SPLITS = [ATTN_WIDTH, 2 * ATTN_WIDTH, 3 * ATTN_WIDTH, 4 * ATTN_WIDTH,
             4 * ATTN_WIDTH + CONV_WIDTH, 4 * ATTN_WIDTH + 2 * CONV_WIDTH]
D_IN = 4 * ATTN_WIDTH + 3 * CONV_WIDTH

kernel_name = "hybrid_stickbreak_conformer_layer"


def _rmsnorm(x, g):
    xf = x.astype(jnp.float32)
    y = xf * lax.rsqrt(jnp.mean(xf * xf, axis=-1, keepdims=True) + RMS_EPS)
    return (y * g.astype(jnp.float32)).astype(x.dtype)


def _layernorm(x, g, b):
    xf = x.astype(jnp.float32)
    mu = jnp.mean(xf, axis=-1, keepdims=True)
    var = jnp.mean(jnp.square(xf - mu), axis=-1, keepdims=True)
    y = (xf - mu) * lax.rsqrt(var + LN_EPS)
    return (y * g.astype(jnp.float32) + b.astype(jnp.float32)).astype(x.dtype)


def _stick_breaking_attention(q, k, v):
    B, S, H, D = q.shape
    n_blk = S // Q_BLOCK
    scale = 1.0 / math.sqrt(D)
    kh = k.transpose(0, 2, 1, 3)
    vh = v.transpose(0, 2, 1, 3)
    qb = q.reshape(B, n_blk, Q_BLOCK, H, D).transpose(1, 0, 3, 2, 4)
    starts = jnp.arange(n_blk, dtype=jnp.int32) * Q_BLOCK
    key_pos = jnp.arange(S, dtype=jnp.int32)

    def one_block(args):
        qblk, t0 = args
        z = jnp.einsum('bhqd,bhkd->bhqk', qblk, kh,
                       preferred_element_type=jnp.float32) * scale
        q_pos = t0 + jnp.arange(Q_BLOCK, dtype=jnp.int32)
        causal = key_pos[None, :] < q_pos[:, None]
        log_1m_beta = jnp.where(causal, -jax.nn.softplus(z), 0.0)
        suffix = lax.cumsum(log_1m_beta, axis=3, reverse=True) - log_1m_beta
        log_a = jax.nn.log_sigmoid(z) + suffix
        a = jnp.where(causal, jnp.exp(log_a), 0.0).astype(vh.dtype)
        o = jnp.einsum('bhqk,bhkd->bhqd', a, vh, preferred_element_type=jnp.float32)
        return o.astype(vh.dtype)

    out = lax.map(one_block, (qb, starts))
    return out.transpose(1, 0, 3, 2, 4).reshape(B, S, H * D)


def _conformer_conv(u, g_glu, conv_w, conv_b, ln_g, ln_b, w_pw2, b_pw2):
    h = u * jax.nn.sigmoid(g_glu)
    C = h.shape[-1]
    h = lax.conv_general_dilated(
        h, conv_w.astype(h.dtype), window_strides=(1,),
        padding=[(CONV_KERNEL - 1, 0)],
        dimension_numbers=('NWC', 'WIO', 'NWC'),
        feature_group_count=C) + conv_b
    h = _layernorm(h, ln_g, ln_b)
    h = jax.nn.silu(h)
    return h @ w_pw2 + b_pw2


def setup_inputs(seed: int = 0) -> dict:
    key = jax.random.key(seed)
    ks = jax.random.split(key, 11)
    x = jax.random.normal(ks[0], (BATCH, SEQ, D_MODEL), jnp.float32)
    g_pre = 1.0 + 0.02 * jax.random.normal(ks[1], (D_MODEL,), jnp.float32)
    w_in = jax.random.normal(ks[2], (D_MODEL, D_IN), jnp.float32) * D_MODEL ** -0.5
    conv_w = jax.random.normal(ks[3], (CONV_KERNEL, 1, CONV_WIDTH), jnp.float32) * CONV_KERNEL ** -0.5
    conv_b = 0.02 * jax.random.normal(ks[4], (CONV_WIDTH,), jnp.float32)
    ln_g = 1.0 + 0.02 * jax.random.normal(ks[5], (CONV_WIDTH,), jnp.float32)
    ln_b = 0.02 * jax.random.normal(ks[6], (CONV_WIDTH,), jnp.float32)
    w_pw2 = jax.random.normal(ks[7], (CONV_WIDTH, CONV_WIDTH), jnp.float32) * CONV_WIDTH ** -0.5
    b_pw2 = 0.02 * jax.random.normal(ks[8], (CONV_WIDTH,), jnp.float32)
    w_out = jax.random.normal(ks[9], (D_MIX, D_MODEL), jnp.float32) * D_MIX ** -0.5
    g_post = 1.0 + 0.02 * jax.random.normal(ks[10], (D_MODEL,), jnp.float32)
    return {"x": x, "g_pre": g_pre, "w_in": w_in, "conv_w": conv_w, "conv_b": conv_b,
            "ln_g": ln_g, "ln_b": ln_b, "w_pw2": w_pw2, "b_pw2": b_pw2,
            "w_out": w_out, "g_post": g_post}


def reference(x, g_pre, w_in, conv_w, conv_b, ln_g, ln_b, w_pw2, b_pw2, w_out, g_post):
    B, S, _ = x.shape
    for _layer in range(DEPTH):
        h = _rmsnorm(x, g_pre)
        proj = h @ w_in
        q, k, v, z_att, u, g_glu, z_conv = jnp.split(proj, IN_SPLITS, axis=-1)
        q = q.reshape(B, S, N_ATTN_HEADS, HEAD_DIM)
        k = k.reshape(B, S, N_ATTN_HEADS, HEAD_DIM)
        v = v.reshape(B, S, N_ATTN_HEADS, HEAD_DIM)
        y_att = _stick_breaking_attention(q, k, v) * jax.nn.silu(z_att)
        y_conv = _conformer_conv(u, g_glu, conv_w, conv_b, ln_g, ln_b,
                                 w_pw2, b_pw2) * jax.nn.silu(z_conv)
        y = jnp.concatenate([y_att, y_conv], axis=-1) @ w_out
        x = x + _rmsnorm(y, g_post)
    return x
```

```python
import functools
import math

import jax
import jax.numpy as jnp
from jax import lax
from jax.experimental import pallas as pl
from jax.experimental.pallas import tpu as pltpu

F32 = jnp.float32
BF16 = jnp.bfloat16

HEAD_DIM = 128
CONV_KERNEL = 31
RMS_EPS = 1e-6
LN_EPS = 1e-5

V7X_LANES = 128
V7X_VMEM_BYTES = 64 * 1024 * 1024

CONV_HALO = 32
CONV_ROW_CHUNK = 32


def _vmem_limit(nbytes):
    return int(min(nbytes * 3 // 2 + (4 << 20), V7X_VMEM_BYTES - (6 << 20)))


def _in_proj_kernel(x_ref, g_ref, w_ref, cs_ref, o_ref, h_scr):
    @pl.when(pl.program_id(1) == 0)
    def _():
        x = x_ref[...]
        ms = jnp.mean(x * x, axis=-1, keepdims=True)
        h_scr[...] = (x * lax.rsqrt(ms + RMS_EPS) * g_ref[...]).astype(BF16)

    acc = jnp.dot(h_scr[...], w_ref[...], preferred_element_type=F32)
    o_ref[...] = (acc * cs_ref[...]).astype(o_ref.dtype)


def _in_proj(x, g_pre, w_in, col_scale, *, tm=1024, tn=512):
    s, d = x.shape
    d_in = w_in.shape[1]
    est = 2 * tm * d * 4 + tm * d * 2 + 2 * d * tn * 2 + 2 * tm * tn * 2
    return pl.pallas_call(
        _in_proj_kernel,
        out_shape=jax.ShapeDtypeStruct((s, d_in), BF16),
        grid=(s // tm, d_in // tn),
        in_specs=[
            pl.BlockSpec((tm, d), lambda i, j: (i, 0)),
            pl.BlockSpec((1, d), lambda i, j: (0, 0)),
            pl.BlockSpec((d, tn), lambda i, j: (0, j)),
            pl.BlockSpec((1, tn), lambda i, j: (0, j)),
        ],
        out_specs=pl.BlockSpec((tm, tn), lambda i, j: (i, j)),
        scratch_shapes=[pltpu.VMEM((tm, d), BF16)],
        compiler_params=pltpu.CompilerParams(
            dimension_semantics=("arbitrary", "arbitrary"),
            vmem_limit_bytes=_vmem_limit(est)),
        name="in_proj",
    )(x, g_pre, w_in, col_scale)


def _softplus(z):
    return jnp.maximum(z, 0.0) + jnp.log1p(jnp.exp(-jnp.abs(z)))


def _attn_kernel(q_ref, k_ref, v_ref, zg_ref, o_ref, *, tq, tk):
    i = pl.program_id(1)
    q = q_ref[...]
    n_kb = (i * tq) // tk + tq // tk
    q_pos = lax.broadcasted_iota(jnp.int32, (tq, tk), 0) + i * tq
    k_off = lax.broadcasted_iota(jnp.int32, (tq, tk), 1)
    later = (lax.broadcasted_iota(jnp.int32, (tk, tk), 0)
             > lax.broadcasted_iota(jnp.int32, (tk, tk), 1)).astype(BF16)

    def body(jj, carry):
        acc, tail = carry
        ks = pl.multiple_of((n_kb - 1 - jj) * tk, tk)
        k = k_ref[pl.ds(ks, tk), :]
        v = v_ref[pl.ds(ks, tk), :]
        z = lax.dot_general(q, k, (((1,), (1,)), ((), ())), preferred_element_type=F32)
        sp = _softplus(z)
        causal = (k_off + ks) < q_pos
        lm = jnp.where(causal, -sp, 0.0)
        lm_hi = lm.astype(BF16)
        lm_lo = (lm - lm_hi.astype(F32)).astype(BF16)
        inner = (jnp.dot(lm_hi, later, preferred_element_type=F32)
                 + jnp.dot(lm_lo, later, preferred_element_type=F32))
        log_a = (z - sp) + inner + tail
        a = jnp.where(causal, jnp.exp(log_a), 0.0)
        acc = acc + jnp.dot(a.astype(BF16), v, preferred_element_type=F32)
        tail = tail + jnp.sum(lm, axis=-1, keepdims=True)
        return acc, tail

    acc, _ = lax.fori_loop(
        0, n_kb, body,
        (jnp.zeros((tq, HEAD_DIM), F32), jnp.zeros((tq, 1), F32)))
    zg = zg_ref[...].astype(F32)
    o_ref[...] = (acc * (zg * jax.nn.sigmoid(zg))).astype(o_ref.dtype)


def _sb_attn(proj, *, n_heads, tq=256, tk=256):
    s = proj.shape[0]
    est = 2 * 2 * s * HEAD_DIM * 2 + 6 * tq * HEAD_DIM * 2 + 12 * tq * tk * 4
    qspec = lambda off: pl.BlockSpec((tq, HEAD_DIM), lambda h, i: (i, off + h))
    kvspec = lambda off: pl.BlockSpec((s, HEAD_DIM), lambda h, i: (0, off + h))
    return pl.pallas_call(
        functools.partial(_attn_kernel, tq=tq, tk=tk),
        out_shape=jax.ShapeDtypeStruct((s, n_heads * HEAD_DIM), BF16),
        grid=(n_heads, s // tq),
        in_specs=[qspec(0), kvspec(n_heads), kvspec(2 * n_heads), qspec(3 * n_heads)],
        out_specs=pl.BlockSpec((tq, HEAD_DIM), lambda h, i: (i, h)),
        compiler_params=pltpu.CompilerParams(
            dimension_semantics=("arbitrary", "arbitrary"),
            vmem_limit_bytes=_vmem_limit(est)),
        name="sb_attn",
    )(proj, proj, proj, proj)


def _conv_kernel(u_ref, g_ref, zc_ref, up_ref, gp_ref, cw_ref, cb_ref, lng_ref, lnb_ref,
                 w2_ref, b2_ref, o_ref, hbuf, cbuf, win, *, tm):
    i = pl.program_id(0)
    c = u_ref.shape[1]
    hbuf[CONV_HALO:, :] = u_ref[...].astype(F32) * jax.nn.sigmoid(g_ref[...].astype(F32))
    h_prev = up_ref[...].astype(F32) * jax.nn.sigmoid(gp_ref[...].astype(F32))
    hbuf[:CONV_HALO, :] = jnp.where(i > 0, h_prev, 0.0)

    first = CONV_HALO - (CONV_KERNEL - 1)

    def row_chunk(rc, _):
        r0 = pl.multiple_of(rc * CONV_ROW_CHUNK, CONV_ROW_CHUNK)
        win[...] = hbuf[pl.ds(r0, CONV_ROW_CHUNK + CONV_HALO), :]
        for cc in range(c // V7X_LANES):
            cols = slice(cc * V7X_LANES, (cc + 1) * V7X_LANES)
            acc = jnp.broadcast_to(cb_ref[:, cols], (CONV_ROW_CHUNK, V7X_LANES))
            for k in range(CONV_KERNEL):
                acc = acc + cw_ref[k:k + 1, cols] * win[first + k:first + k + CONV_ROW_CHUNK, cols]
            cbuf[pl.ds(r0, CONV_ROW_CHUNK), cols] = acc
        return 0

    lax.fori_loop(0, tm // CONV_ROW_CHUNK, row_chunk, 0)

    hc = cbuf[...]
    mu = jnp.mean(hc, axis=-1, keepdims=True)
    d = hc - mu
    var = jnp.mean(d * d, axis=-1, keepdims=True)
    y = d * lax.rsqrt(var + LN_EPS) * lng_ref[...] + lnb_ref[...]
    y = y * jax.nn.sigmoid(y)
    p = jnp.dot(y.astype(BF16), w2_ref[...], preferred_element_type=F32) + b2_ref[...]
    zc = zc_ref[...].astype(F32)
    o_ref[...] = (p * (zc * jax.nn.sigmoid(zc))).astype(o_ref.dtype)


def _conv_glu(proj, conv_w, conv_b, ln_g, ln_b, w_pw2, b_pw2, *, u_blk, tm=512):
    s = proj.shape[0]
    c = conv_w.shape[1]
    halo_blocks = tm // CONV_HALO
    est = (2 * 3 * tm * c * 2 + (tm + CONV_HALO) * c * 4 + tm * c * 4 + 2 * c * c * 2
           + 2 * tm * c * 2 + 4 * tm * c * 4)
    cur = lambda off: pl.BlockSpec((tm, c), lambda i: (i, off))
    prev = lambda off: pl.BlockSpec(
        (CONV_HALO, c), lambda i: (jnp.maximum(i * halo_blocks - 1, 0), off))
    row = lambda n: pl.BlockSpec((n, c), lambda i: (0, 0))
    return pl.pallas_call(
        functools.partial(_conv_kernel, tm=tm),
        out_shape=jax.ShapeDtypeStruct((s, c), BF16),
        grid=(s // tm,),
        in_specs=[cur(u_blk), cur(u_blk + 1), cur(u_blk + 2), prev(u_blk), prev(u_blk + 1),
                  row(CONV_KERNEL), row(1), row(1), row(1),
                  pl.BlockSpec((c, c), lambda i: (0, 0)), row(1)],
        out_specs=pl.BlockSpec((tm, c), lambda i: (i, 0)),
        scratch_shapes=[pltpu.VMEM((tm + CONV_HALO, c), F32), pltpu.VMEM((tm, c), F32),
                        pltpu.VMEM((CONV_ROW_CHUNK + CONV_HALO, c), F32)],
        compiler_params=pltpu.CompilerParams(
            dimension_semantics=("arbitrary",),
            vmem_limit_bytes=_vmem_limit(est)),
        name="conv_glu",
    )(proj, proj, proj, proj, proj, conv_w, conv_b, ln_g, ln_b, w_pw2, b_pw2)


def _out_proj_kernel(ya_ref, yc_ref, wa_ref, wc_ref, x_ref, g_ref, o_ref):
    y = (jnp.dot(ya_ref[...], wa_ref[...], preferred_element_type=F32)
         + jnp.dot(yc_ref[...], wc_ref[...], preferred_element_type=F32))
    ms = jnp.mean(y * y, axis=-1, keepdims=True)
    o_ref[...] = x_ref[...] + y * lax.rsqrt(ms + RMS_EPS) * g_ref[...]


def _out_proj(y_att, y_conv, w_out, x, g_post, *, tm=512):
    s, d = x.shape
    ca, cc = y_att.shape[1], y_conv.shape[1]
    est = 2 * (ca + cc) * d * 2 + 2 * tm * (ca + cc) * 2 + 4 * tm * d * 4 + 2 * tm * d * 4
    return pl.pallas_call(
        _out_proj_kernel,
        out_shape=jax.ShapeDtypeStruct((s, d), x.dtype),
        grid=(s // tm,),
        in_specs=[
            pl.BlockSpec((tm, ca), lambda i: (i, 0)),
            pl.BlockSpec((tm, cc), lambda i: (i, 0)),
            pl.BlockSpec((ca, d), lambda i: (0, 0)),
            pl.BlockSpec((cc, d), lambda i: (ca // cc, 0)),
            pl.BlockSpec((tm, d), lambda i: (i, 0)),
            pl.BlockSpec((1, d), lambda i: (0, 0)),
        ],
        out_specs=pl.BlockSpec((tm, d), lambda i: (i, 0)),
        compiler_params=pltpu.CompilerParams(
            dimension_semantics=("arbitrary",),
            vmem_limit_bytes=_vmem_limit(est)),
        name="out_proj",
    )(y_att, y_conv, w_out, w_out, x, g_post)


def kernel(x, g_pre, w_in, conv_w, conv_b, ln_g, ln_b, w_pw2, b_pw2, w_out, g_post):
    b, s, d = x.shape
    assert b == 1, "rows of different batch entries must not share a causal history"
    c = conv_w.shape[-1]
    aw = w_out.shape[0] - c
    n_heads = aw // HEAD_DIM
    assert w_in.shape[1] == 4 * aw + 3 * c and aw == c

    col_scale = jnp.concatenate(
        [jnp.full((aw,), 1.0 / math.sqrt(HEAD_DIM), F32), jnp.ones((w_in.shape[1] - aw,), F32)])

    x2 = x.reshape(s, d)
    proj = _in_proj(x2, g_pre.reshape(1, d), w_in.astype(BF16), col_scale.reshape(1, -1))
    y_att = _sb_attn(proj, n_heads=n_heads)
    y_conv = _conv_glu(proj, conv_w.reshape(CONV_KERNEL, c), conv_b.reshape(1, c),
                       ln_g.reshape(1, c), ln_b.reshape(1, c), w_pw2.astype(BF16),
                       b_pw2.reshape(1, c), u_blk=(4 * aw) // c)
    out = _out_proj(y_att, y_conv, w_out.astype(BF16), x2, g_post.reshape(1, d))
    return out.reshape(b, s, d)
```

```python
import functools
import math

import jax
import jax.numpy as jnp
from jax import lax
from jax.experimental import pallas as pl
from jax.experimental.pallas import tpu as pltpu

F32 = jnp.float32
BF16 = jnp.bfloat16

HEAD_DIM = 128
CONV_KERNEL = 31
RMS_EPS = 1e-6
LN_EPS = 1e-5

V7X_LANES = 128
V7X_VMEM_BYTES = 64 * 1024 * 1024

CONV_HALO = 32
CONV_ROW_CHUNK = 32


def _vmem_limit(nbytes):
    return int(min(nbytes * 3 // 2 + (4 << 20), V7X_VMEM_BYTES - (6 << 20)))


def _in_proj_kernel(x_ref, g_ref, w_ref, cs_ref, o_ref, h_scr):
    @pl.when(pl.program_id(1) == 0)
    def _():
        x = x_ref[...]
        ms = jnp.mean(x * x, axis=-1, keepdims=True)
        h_scr[...] = (x * lax.rsqrt(ms + RMS_EPS) * g_ref[...]).astype(BF16)

    acc = jnp.dot(h_scr[...], w_ref[...], preferred_element_type=F32)
    o_ref[...] = (acc * cs_ref[...]).astype(o_ref.dtype)


def _in_proj(x, g_pre, w_in, col_scale, *, tm=1024, tn=512):
    s, d = x.shape
    d_in = w_in.shape[1]
    est = 2 * tm * d * 4 + tm * d * 2 + 2 * d * tn * 2 + 2 * tm * tn * 2
    return pl.pallas_call(
        _in_proj_kernel,
        out_shape=jax.ShapeDtypeStruct((s, d_in), BF16),
        grid=(s // tm, d_in // tn),
        in_specs=[
            pl.BlockSpec((tm, d), lambda i, j: (i, 0)),
            pl.BlockSpec((1, d), lambda i, j: (0, 0)),
            pl.BlockSpec((d, tn), lambda i, j: (0, j)),
            pl.BlockSpec((1, tn), lambda i, j: (0, j)),
        ],
        out_specs=pl.BlockSpec((tm, tn), lambda i, j: (i, j)),
        scratch_shapes=[pltpu.VMEM((tm, d), BF16)],
        compiler_params=pltpu.CompilerParams(
            dimension_semantics=("arbitrary", "arbitrary"),
            vmem_limit_bytes=_vmem_limit(est)),
        name="in_proj",
    )(x, g_pre, w_in, col_scale)


F32_EXP_UNDERFLOW_LOG = -105.0


def _softplus(z):
    return jnp.maximum(z, 0.0) + jnp.log(1.0 + jnp.exp(-jnp.abs(z)))


def _attn_kernel(q_ref, k_ref, v_ref, zg_ref, o_ref, acc_ref, tail_ref, *, tq):
    i = pl.program_id(1)
    tk = tq
    q = q_ref[...]
    row = lax.broadcasted_iota(jnp.int32, (tk, tk), 0)
    col = lax.broadcasted_iota(jnp.int32, (tk, tk), 1)
    later = (row > col).astype(BF16)
    acc_ref[...] = jnp.zeros_like(acc_ref)
    tail_ref[...] = jnp.zeros_like(tail_ref)

    def key_block(j, diagonal):
        ks = pl.multiple_of(j * tk, tk)
        k = k_ref[pl.ds(ks, tk), :]
        v = v_ref[pl.ds(ks, tk), :]
        z = lax.dot_general(q, k, (((1,), (1,)), ((), ())), preferred_element_type=F32)
        sp = _softplus(z)
        log_beta = z - sp
        if diagonal:
            causal = col < row
            sp = jnp.where(causal, sp, 0.0)
        sp_hi = sp.astype(BF16)
        sp_lo = (sp - sp_hi.astype(F32)).astype(BF16)
        inner = (jnp.dot(sp_hi, later, preferred_element_type=F32)
                 + jnp.dot(sp_lo, later, preferred_element_type=F32))
        tail = tail_ref[...]
        a = jnp.exp(log_beta - inner + tail)
        if diagonal:
            a = jnp.where(causal, a, 0.0)
        acc_ref[...] += jnp.dot(a.astype(BF16), v, preferred_element_type=F32)
        tail = tail - jnp.sum(sp, axis=-1, keepdims=True)
        tail_ref[...] = tail
        return tail

    key_block(i, diagonal=True)

    def more(state):
        j, live = state
        return jnp.logical_and(j >= 0, live)

    def walk(state):
        j, _ = state
        tail = key_block(j, diagonal=False)
        return j - 1, jnp.max(tail) >= F32_EXP_UNDERFLOW_LOG

    lax.while_loop(more, walk, (i - 1, True))
    zg = zg_ref[...].astype(F32)
    o_ref[...] = (acc_ref[...] * (zg * jax.nn.sigmoid(zg))).astype(o_ref.dtype)


def _sb_attn(proj, *, n_heads, tq=256):
    s = proj.shape[0]
    est = 2 * 2 * s * HEAD_DIM * 2 + 6 * tq * HEAD_DIM * 2 + 12 * tq * tq * 4
    qspec = lambda off: pl.BlockSpec((tq, HEAD_DIM), lambda h, i: (i, off + h))
    kvspec = lambda off: pl.BlockSpec((s, HEAD_DIM), lambda h, i: (0, off + h))
    return pl.pallas_call(
        functools.partial(_attn_kernel, tq=tq),
        out_shape=jax.ShapeDtypeStruct((s, n_heads * HEAD_DIM), BF16),
        grid=(n_heads, s // tq),
        in_specs=[qspec(0), kvspec(n_heads), kvspec(2 * n_heads), qspec(3 * n_heads)],
        out_specs=pl.BlockSpec((tq, HEAD_DIM), lambda h, i: (i, h)),
        scratch_shapes=[pltpu.VMEM((tq, HEAD_DIM), F32), pltpu.VMEM((tq, 1), F32)],
        compiler_params=pltpu.CompilerParams(
            dimension_semantics=("arbitrary", "arbitrary"),
            vmem_limit_bytes=_vmem_limit(est)),
        name="sb_attn",
    )(proj, proj, proj, proj)


def _conv_kernel(u_ref, g_ref, zc_ref, up_ref, gp_ref, cw_ref, cb_ref, lng_ref, lnb_ref,
                 w2_ref, b2_ref, o_ref, hbuf, cbuf, win, *, tm):
    i = pl.program_id(0)
    c = u_ref.shape[1]
    hbuf[CONV_HALO:, :] = u_ref[...].astype(F32) * jax.nn.sigmoid(g_ref[...].astype(F32))
    h_prev = up_ref[...].astype(F32) * jax.nn.sigmoid(gp_ref[...].astype(F32))
    hbuf[:CONV_HALO, :] = jnp.where(i > 0, h_prev, 0.0)

    first = CONV_HALO - (CONV_KERNEL - 1)

    def row_chunk(rc, _):
        r0 = pl.multiple_of(rc * CONV_ROW_CHUNK, CONV_ROW_CHUNK)
        win[...] = hbuf[pl.ds(r0, CONV_ROW_CHUNK + CONV_HALO), :]
        for cc in range(c // V7X_LANES):
            cols = slice(cc * V7X_LANES, (cc + 1) * V7X_LANES)
            acc = jnp.broadcast_to(cb_ref[:, cols], (CONV_ROW_CHUNK, V7X_LANES))
            for k in range(CONV_KERNEL):
                acc = acc + cw_ref[k:k + 1, cols] * win[first + k:first + k + CONV_ROW_CHUNK, cols]
            cbuf[pl.ds(r0, CONV_ROW_CHUNK), cols] = acc
        return 0

    lax.fori_loop(0, tm // CONV_ROW_CHUNK, row_chunk, 0)

    hc = cbuf[...]
    mu = jnp.mean(hc, axis=-1, keepdims=True)
    d = hc - mu
    var = jnp.mean(d * d, axis=-1, keepdims=True)
    y = d * lax.rsqrt(var + LN_EPS) * lng_ref[...] + lnb_ref[...]
    y = y * jax.nn.sigmoid(y)
    p = jnp.dot(y.astype(BF16), w2_ref[...], preferred_element_type=F32) + b2_ref[...]
    zc = zc_ref[...].astype(F32)
    o_ref[...] = (p * (zc * jax.nn.sigmoid(zc))).astype(o_ref.dtype)


def _conv_glu(proj, conv_w, conv_b, ln_g, ln_b, w_pw2, b_pw2, *, u_blk, tm=512):
    s = proj.shape[0]
    c = conv_w.shape[1]
    halo_blocks = tm // CONV_HALO
    est = (2 * 3 * tm * c * 2 + (tm + CONV_HALO) * c * 4 + tm * c * 4 + 2 * c * c * 2
           + 2 * tm * c * 2 + 4 * tm * c * 4)
    cur = lambda off: pl.BlockSpec((tm, c), lambda i: (i, off))
    prev = lambda off: pl.BlockSpec(
        (CONV_HALO, c), lambda i: (jnp.maximum(i * halo_blocks - 1, 0), off))
    row = lambda n: pl.BlockSpec((n, c), lambda i: (0, 0))
    return pl.pallas_call(
        functools.partial(_conv_kernel, tm=tm),
        out_shape=jax.ShapeDtypeStruct((s, c), BF16),
        grid=(s // tm,),
        in_specs=[cur(u_blk), cur(u_blk + 1), cur(u_blk + 2), prev(u_blk), prev(u_blk + 1),
                  row(CONV_KERNEL), row(1), row(1), row(1),
                  pl.BlockSpec((c, c), lambda i: (0, 0)), row(1)],
        out_specs=pl.BlockSpec((tm, c), lambda i: (i, 0)),
        scratch_shapes=[pltpu.VMEM((tm + CONV_HALO, c), F32), pltpu.VMEM((tm, c), F32),
                        pltpu.VMEM((CONV_ROW_CHUNK + CONV_HALO, c), F32)],
        compiler_params=pltpu.CompilerParams(
            dimension_semantics=("arbitrary",),
            vmem_limit_bytes=_vmem_limit(est)),
        name="conv_glu",
    )(proj, proj, proj, proj, proj, conv_w, conv_b, ln_g, ln_b, w_pw2, b_pw2)


def _out_proj_kernel(ya_ref, yc_ref, wa_ref, wc_ref, x_ref, g_ref, o_ref):
    y = (jnp.dot(ya_ref[...], wa_ref[...], preferred_element_type=F32)
         + jnp.dot(yc_ref[...], wc_ref[...], preferred_element_type=F32))
    ms = jnp.mean(y * y, axis=-1, keepdims=True)
    o_ref[...] = x_ref[...] + y * lax.rsqrt(ms + RMS_EPS) * g_ref[...]


def _out_proj(y_att, y_conv, w_out, x, g_post, *, tm=512):
    s, d = x.shape
    ca, cc = y_att.shape[1], y_conv.shape[1]
    est = 2 * (ca + cc) * d * 2 + 2 * tm * (ca + cc) * 2 + 4 * tm * d * 4 + 2 * tm * d * 4
    return pl.pallas_call(
        _out_proj_kernel,
        out_shape=jax.ShapeDtypeStruct((s, d), x.dtype),
        grid=(s // tm,),
        in_specs=[
            pl.BlockSpec((tm, ca), lambda i: (i, 0)),
            pl.BlockSpec((tm, cc), lambda i: (i, 0)),
            pl.BlockSpec((ca, d), lambda i: (0, 0)),
            pl.BlockSpec((cc, d), lambda i: (ca // cc, 0)),
            pl.BlockSpec((tm, d), lambda i: (i, 0)),
            pl.BlockSpec((1, d), lambda i: (0, 0)),
        ],
        out_specs=pl.BlockSpec((tm, d), lambda i: (i, 0)),
        compiler_params=pltpu.CompilerParams(
            dimension_semantics=("arbitrary",),
            vmem_limit_bytes=_vmem_limit(est)),
        name="out_proj",
    )(y_att, y_conv, w_out, w_out, x, g_post)


def kernel(x, g_pre, w_in, conv_w, conv_b, ln_g, ln_b, w_pw2, b_pw2, w_out, g_post):
    b, s, d = x.shape
    assert b == 1, "rows of different batch entries must not share a causal history"
    c = conv_w.shape[-1]
    aw = w_out.shape[0] - c
    n_heads = aw // HEAD_DIM
    assert w_in.shape[1] == 4 * aw + 3 * c and aw == c

    col_scale = jnp.concatenate(
        [jnp.full((aw,), 1.0 / math.sqrt(HEAD_DIM), F32), jnp.ones((w_in.shape[1] - aw,), F32)])

    x2 = x.reshape(s, d)
    proj = _in_proj(x2, g_pre.reshape(1, d), w_in.astype(BF16), col_scale.reshape(1, -1))
    y_att = _sb_attn(proj, n_heads=n_heads)
    y_conv = _conv_glu(proj, conv_w.reshape(CONV_KERNEL, c), conv_b.reshape(1, c),
                       ln_g.reshape(1, c), ln_b.reshape(1, c), w_pw2.astype(BF16),
                       b_pw2.reshape(1, c), u_blk=(4 * aw) // c)
    out = _out_proj(y_att, y_conv, w_out.astype(BF16), x2, g_post.reshape(1, d))
    return out.reshape(b, s, d)
```

```python
import functools
import math

import jax
import jax.numpy as jnp
from jax import lax
from jax.experimental import pallas as pl
from jax.experimental.pallas import tpu as pltpu

F32 = jnp.float32
BF16 = jnp.bfloat16

HEAD_DIM = 128
CONV_KERNEL = 31
RMS_EPS = 1e-6
LN_EPS = 1e-5

V7X_LANES = 128
V7X_VMEM_BYTES = 64 * 1024 * 1024

CONV_HALO = 32
CONV_ROW_CHUNK = 64


def _vmem_limit(nbytes):
    return int(min(nbytes * 3 // 2 + (4 << 20), V7X_VMEM_BYTES - (6 << 20)))


def _in_proj_kernel(x_ref, g_ref, w_ref, cs_ref, o_ref, h_scr):
    @pl.when(pl.program_id(1) == 0)
    def _():
        x = x_ref[...]
        ms = jnp.mean(x * x, axis=-1, keepdims=True)
        h_scr[...] = (x * lax.rsqrt(ms + RMS_EPS) * g_ref[...]).astype(BF16)

    acc = jnp.dot(h_scr[...], w_ref[...], preferred_element_type=F32)
    o_ref[...] = (acc * cs_ref[...]).astype(o_ref.dtype)


def _in_proj(x, g_pre, w_in, col_scale, *, tm=1024, tn=1024):
    s, d = x.shape
    d_in = w_in.shape[1]
    est = 2 * tm * d * 4 + tm * d * 2 + 2 * d * tn * 2 + 2 * tm * tn * 2
    return pl.pallas_call(
        _in_proj_kernel,
        out_shape=jax.ShapeDtypeStruct((s, d_in), BF16),
        grid=(s // tm, d_in // tn),
        in_specs=[
            pl.BlockSpec((tm, d), lambda i, j: (i, 0)),
            pl.BlockSpec((1, d), lambda i, j: (0, 0)),
            pl.BlockSpec((d, tn), lambda i, j: (0, j)),
            pl.BlockSpec((1, tn), lambda i, j: (0, j)),
        ],
        out_specs=pl.BlockSpec((tm, tn), lambda i, j: (i, j)),
        scratch_shapes=[pltpu.VMEM((tm, d), BF16)],
        compiler_params=pltpu.CompilerParams(
            dimension_semantics=("arbitrary", "arbitrary"),
            vmem_limit_bytes=_vmem_limit(est)),
        name="in_proj",
    )(x, g_pre, w_in, col_scale)


LOG2_E = math.log2(math.e)
F32_EXP2_UNDERFLOW = -151.5


def _softplus2(z2):
    return jnp.maximum(z2, 0.0) + jnp.log(1.0 + jnp.exp2(-jnp.abs(z2))) * LOG2_E


def _attn_kernel(q_ref, k_ref, v_ref, zg_ref, o_ref, acc_ref, tail_ref, *, tb, n_sub):
    i = pl.program_id(1)
    row = lax.broadcasted_iota(jnp.int32, (tb, tb), 0)
    col = lax.broadcasted_iota(jnp.int32, (tb, tb), 1)
    later = (row > col).astype(BF16)
    causal = col < row

    def key_block(q, j, tail, diagonal):
        ks = pl.multiple_of(j * tb, tb)
        k = k_ref[pl.ds(ks, tb), :]
        v = v_ref[pl.ds(ks, tb), :]
        z2 = lax.dot_general(q, k, (((1,), (1,)), ((), ())), preferred_element_type=F32)
        sp = _softplus2(z2)
        log_beta = z2 - sp
        if diagonal:
            sp = jnp.where(causal, sp, 0.0)
        sp_hi = sp.astype(BF16)
        sp_lo = (sp - sp_hi.astype(F32)).astype(BF16)
        inner = (jnp.dot(sp_hi, later, preferred_element_type=F32)
                 + jnp.dot(sp_lo, later, preferred_element_type=F32))
        if diagonal:
            a = jnp.where(causal, jnp.exp2(log_beta - inner), 0.0)
            tail = -jnp.sum(sp, axis=-1, keepdims=True)
        else:
            a = jnp.exp2(log_beta - inner + tail)
            tail = tail - jnp.sum(sp, axis=-1, keepdims=True)
        pv = jnp.dot(a.astype(BF16), v, preferred_element_type=F32)
        return pv, tail

    for sb in range(n_sub):
        rows = slice(sb * tb, (sb + 1) * tb)
        qb = i * n_sub + sb
        q = q_ref[rows, :]
        pv_d, tail = key_block(q, qb, None, diagonal=True)
        tail = tail - jnp.where(qb > 0, 0.0, 1e30)
        pv_p, tail = key_block(q, jnp.maximum(qb - 1, 0), tail, diagonal=False)
        acc_ref[rows, :] = pv_d + pv_p
        tail_ref[rows, :] = tail

    def more(state):
        step, live = state
        return jnp.logical_and((i + 1) * n_sub - 3 - step >= 0, live)

    def walk(state):
        step, _ = state
        for sb in range(n_sub):
            rows = slice(sb * tb, (sb + 1) * tb)
            j = i * n_sub + sb - 2 - step

            @pl.when(j >= 0)
            def _():
                pv, tail = key_block(q_ref[rows, :], j, tail_ref[rows, :], diagonal=False)
                acc_ref[rows, :] += pv
                tail_ref[rows, :] = tail
        return step + 1, jnp.max(tail_ref[...]) >= F32_EXP2_UNDERFLOW

    lax.while_loop(more, walk, (0, jnp.max(tail_ref[...]) >= F32_EXP2_UNDERFLOW))

    zg = zg_ref[...].astype(F32)
    o_ref[...] = (acc_ref[...] * (zg * jax.nn.sigmoid(zg))).astype(o_ref.dtype)


def _sb_attn(proj, *, n_heads, tb=256, n_sub=4):
    s = proj.shape[0]
    tq = tb * n_sub
    est = 2 * 2 * s * HEAD_DIM * 2 + 6 * tq * HEAD_DIM * 2 + n_sub * 16 * tb * tb * 4
    qspec = lambda off: pl.BlockSpec((tq, HEAD_DIM), lambda h, i: (i, off + h))
    kvspec = lambda off: pl.BlockSpec((s, HEAD_DIM), lambda h, i: (0, off + h))
    return pl.pallas_call(
        functools.partial(_attn_kernel, tb=tb, n_sub=n_sub),
        out_shape=jax.ShapeDtypeStruct((s, n_heads * HEAD_DIM), BF16),
        grid=(n_heads, s // tq),
        in_specs=[qspec(0), kvspec(n_heads), kvspec(2 * n_heads), qspec(3 * n_heads)],
        out_specs=pl.BlockSpec((tq, HEAD_DIM), lambda h, i: (i, h)),
        scratch_shapes=[pltpu.VMEM((tq, HEAD_DIM), F32), pltpu.VMEM((tq, 1), F32)],
        compiler_params=pltpu.CompilerParams(
            dimension_semantics=("arbitrary", "arbitrary"),
            vmem_limit_bytes=_vmem_limit(est)),
        name="sb_attn",
    )(proj, proj, proj, proj)


def _conv_kernel(u_ref, g_ref, zc_ref, up_ref, gp_ref, cw_ref, cb_ref, lng_ref, lnb_ref,
                 w2_ref, b2_ref, o_ref, hbuf, cbuf, *, tm):
    i = pl.program_id(0)
    c = u_ref.shape[1]
    hbuf[CONV_HALO:, :] = u_ref[...].astype(F32) * jax.nn.sigmoid(g_ref[...].astype(F32))
    h_prev = up_ref[...].astype(F32) * jax.nn.sigmoid(gp_ref[...].astype(F32))
    hbuf[:CONV_HALO, :] = jnp.where(i > 0, h_prev, 0.0)

    first = CONV_HALO - (CONV_KERNEL - 1)
    n_win = CONV_ROW_CHUNK + CONV_HALO

    def row_chunk(rc, _):
        r0 = pl.multiple_of(rc * CONV_ROW_CHUNK, CONV_ROW_CHUNK)
        for cc in range(c // V7X_LANES):
            cols = slice(cc * V7X_LANES, (cc + 1) * V7X_LANES)
            window = hbuf[pl.ds(r0, n_win), cols]
            acc = jnp.broadcast_to(cb_ref[:, cols], (CONV_ROW_CHUNK, V7X_LANES))
            for sub in range(8):
                shifted = window if sub == 0 else pltpu.roll(window, n_win - sub, axis=0)
                for base in range(0, CONV_HALO + 1, 8):
                    k = base + sub - first
                    if 0 <= k < CONV_KERNEL:
                        acc = acc + cw_ref[k:k + 1, cols] * shifted[base:base + CONV_ROW_CHUNK, :]
            cbuf[pl.ds(r0, CONV_ROW_CHUNK), cols] = acc
        return 0

    lax.fori_loop(0, tm // CONV_ROW_CHUNK, row_chunk, 0)

    hc = cbuf[...]
    mu = jnp.mean(hc, axis=-1, keepdims=True)
    d = hc - mu
    var = jnp.mean(d * d, axis=-1, keepdims=True)
    y = d * lax.rsqrt(var + LN_EPS) * lng_ref[...] + lnb_ref[...]
    y = y * jax.nn.sigmoid(y)
    p = jnp.dot(y.astype(BF16), w2_ref[...], preferred_element_type=F32) + b2_ref[...]
    zc = zc_ref[...].astype(F32)
    o_ref[...] = (p * (zc * jax.nn.sigmoid(zc))).astype(o_ref.dtype)


def _conv_glu(proj, conv_w, conv_b, ln_g, ln_b, w_pw2, b_pw2, *, u_blk, tm=512):
    s = proj.shape[0]
    c = conv_w.shape[1]
    halo_blocks = tm // CONV_HALO
    est = (2 * 3 * tm * c * 2 + (tm + CONV_HALO) * c * 4 + tm * c * 4 + 2 * c * c * 2
           + 2 * tm * c * 2 + 4 * tm * c * 4)
    cur = lambda off: pl.BlockSpec((tm, c), lambda i: (i, off))
    prev = lambda off: pl.BlockSpec(
        (CONV_HALO, c), lambda i: (jnp.maximum(i * halo_blocks - 1, 0), off))
    row = lambda n: pl.BlockSpec((n, c), lambda i: (0, 0))
    return pl.pallas_call(
        functools.partial(_conv_kernel, tm=tm),
        out_shape=jax.ShapeDtypeStruct((s, c), BF16),
        grid=(s // tm,),
        in_specs=[cur(u_blk), cur(u_blk + 1), cur(u_blk + 2), prev(u_blk), prev(u_blk + 1),
                  row(CONV_KERNEL), row(1), row(1), row(1),
                  pl.BlockSpec((c, c), lambda i: (0, 0)), row(1)],
        out_specs=pl.BlockSpec((tm, c), lambda i: (i, 0)),
        scratch_shapes=[pltpu.VMEM((tm + CONV_HALO, c), F32), pltpu.VMEM((tm, c), F32)],
        compiler_params=pltpu.CompilerParams(
            dimension_semantics=("arbitrary",),
            vmem_limit_bytes=_vmem_limit(est)),
        name="conv_glu",
    )(proj, proj, proj, proj, proj, conv_w, conv_b, ln_g, ln_b, w_pw2, b_pw2)


def _out_proj_kernel(ya_ref, yc_ref, wa_ref, wc_ref, x_ref, g_ref, o_ref):
    y = (jnp.dot(ya_ref[...], wa_ref[...], preferred_element_type=F32)
         + jnp.dot(yc_ref[...], wc_ref[...], preferred_element_type=F32))
    ms = jnp.mean(y * y, axis=-1, keepdims=True)
    o_ref[...] = x_ref[...] + y * lax.rsqrt(ms + RMS_EPS) * g_ref[...]


def _out_proj(y_att, y_conv, w_out, x, g_post, *, tm=512):
    s, d = x.shape
    ca, cc = y_att.shape[1], y_conv.shape[1]
    est = 2 * (ca + cc) * d * 2 + 2 * tm * (ca + cc) * 2 + 4 * tm * d * 4 + 2 * tm * d * 4
    return pl.pallas_call(
        _out_proj_kernel,
        out_shape=jax.ShapeDtypeStruct((s, d), x.dtype),
        grid=(s // tm,),
        in_specs=[
            pl.BlockSpec((tm, ca), lambda i: (i, 0)),
            pl.BlockSpec((tm, cc), lambda i: (i, 0)),
            pl.BlockSpec((ca, d), lambda i: (0, 0)),
            pl.BlockSpec((cc, d), lambda i: (ca // cc, 0)),
            pl.BlockSpec((tm, d), lambda i: (i, 0)),
            pl.BlockSpec((1, d), lambda i: (0, 0)),
        ],
        out_specs=pl.BlockSpec((tm, d), lambda i: (i, 0)),
        compiler_params=pltpu.CompilerParams(
            dimension_semantics=("arbitrary",),
            vmem_limit_bytes=_vmem_limit(est)),
        name="out_proj",
    )(y_att, y_conv, w_out, w_out, x, g_post)


def kernel(x, g_pre, w_in, conv_w, conv_b, ln_g, ln_b, w_pw2, b_pw2, w_out, g_post):
    b, s, d = x.shape
    assert b == 1, "rows of different batch entries must not share a causal history"
    c = conv_w.shape[-1]
    aw = w_out.shape[0] - c
    n_heads = aw // HEAD_DIM
    assert w_in.shape[1] == 4 * aw + 3 * c and aw == c

    col_scale = jnp.concatenate(
        [jnp.full((aw,), LOG2_E / math.sqrt(HEAD_DIM), F32), jnp.ones((w_in.shape[1] - aw,), F32)])

    x2 = x.reshape(s, d)
    proj = _in_proj(x2, g_pre.reshape(1, d), w_in.astype(BF16), col_scale.reshape(1, -1))
    y_att = _sb_attn(proj, n_heads=n_heads)
    y_conv = _conv_glu(proj, conv_w.reshape(CONV_KERNEL, c), conv_b.reshape(1, c),
                       ln_g.reshape(1, c), ln_b.reshape(1, c), w_pw2.astype(BF16),
                       b_pw2.reshape(1, c), u_blk=(4 * aw) // c)
    out = _out_proj(y_att, y_conv, w_out.astype(BF16), x2, g_post.reshape(1, d))
    return out.reshape(b, s, d)
```

```python
import functools
import math

import jax
import jax.numpy as jnp
from jax import lax
from jax.experimental import pallas as pl
from jax.experimental.pallas import tpu as pltpu

F32 = jnp.float32
BF16 = jnp.bfloat16

HEAD_DIM = 128
CONV_KERNEL = 31
RMS_EPS = 1e-6
LN_EPS = 1e-5

V7X_LANES = 128
V7X_VMEM_BYTES = 64 * 1024 * 1024

CONV_HALO = 32
CONV_ROW_CHUNK = 128


def _vmem_limit(nbytes):
    return int(min(nbytes * 3 // 2 + (4 << 20), V7X_VMEM_BYTES - (6 << 20)))


def _in_proj_kernel(xa_ref, xb_ref, g_ref, w_ref, cs_ref, o_ref, h_scr):
    @pl.when(pl.program_id(1) == 0)
    def _():
        half = xa_ref.shape[0]
        for x_ref, rows in ((xa_ref, slice(0, half)), (xb_ref, slice(half, 2 * half))):
            x = x_ref[...]
            ms = jnp.mean(x * x, axis=-1, keepdims=True)
            h_scr[rows, :] = (x * lax.rsqrt(ms + RMS_EPS) * g_ref[...]).astype(BF16)

    acc = jnp.dot(h_scr[...], w_ref[...], preferred_element_type=F32)
    o_ref[...] = (acc * cs_ref[...]).astype(o_ref.dtype)


def _in_proj(x, g_pre, w_in, col_scale, *, tm=1024, tn=1024):
    s, d = x.shape
    d_in = w_in.shape[1]
    n_i, n_j = s // tm, d_in // tn
    est = 2 * tm * d * 4 + tm * d * 2 + 2 * d * tn * 2 + 2 * tm * tn * 2

    def x_half(which):
        lead = 2 - which
        return pl.BlockSpec(
            (tm // 2, d),
            lambda i, j: (2 * jnp.minimum(i + (j >= n_j - lead).astype(jnp.int32), n_i - 1) + which, 0))

    return pl.pallas_call(
        _in_proj_kernel,
        out_shape=jax.ShapeDtypeStruct((s, d_in), BF16),
        grid=(n_i, n_j),
        in_specs=[
            x_half(0),
            x_half(1),
            pl.BlockSpec((1, d), lambda i, j: (0, 0)),
            pl.BlockSpec((d, tn), lambda i, j: (0, j)),
            pl.BlockSpec((1, tn), lambda i, j: (0, j)),
        ],
        out_specs=pl.BlockSpec((tm, tn), lambda i, j: (i, j)),
        scratch_shapes=[pltpu.VMEM((tm, d), BF16)],
        compiler_params=pltpu.CompilerParams(
            dimension_semantics=("arbitrary", "arbitrary"),
            vmem_limit_bytes=_vmem_limit(est)),
        name="in_proj",
    )(x, x, g_pre, w_in, col_scale)


LOG2_E = math.log2(math.e)
F32_EXP2_UNDERFLOW = -151.5


def _softplus2(z2):
    return jnp.maximum(z2, 0.0) + jnp.log(1.0 + jnp.exp2(-jnp.abs(z2))) * LOG2_E


def _attn_kernel(q_ref, k_ref, v_ref, zg_ref, o_ref, acc_ref, tail_ref, *, tb, n_sub):
    i = pl.program_id(1)
    row = lax.broadcasted_iota(jnp.int32, (tb, tb), 0)
    col = lax.broadcasted_iota(jnp.int32, (tb, tb), 1)
    later = (row > col).astype(BF16)
    causal = col < row

    def key_block(q, j, tail, diagonal):
        ks = pl.multiple_of(j * tb, tb)
        k = k_ref[pl.ds(ks, tb), :]
        v = v_ref[pl.ds(ks, tb), :]
        z2 = lax.dot_general(q, k, (((1,), (1,)), ((), ())), preferred_element_type=F32)
        sp = _softplus2(z2)
        log_beta = z2 - sp
        if diagonal:
            sp = jnp.where(causal, sp, 0.0)
        sp_hi = sp.astype(BF16)
        sp_lo = (sp - sp_hi.astype(F32)).astype(BF16)
        inner = (jnp.dot(sp_hi, later, preferred_element_type=F32)
                 + jnp.dot(sp_lo, later, preferred_element_type=F32))
        if diagonal:
            a = jnp.where(causal, jnp.exp2(log_beta - inner), 0.0)
            tail = -jnp.sum(sp, axis=-1, keepdims=True)
        else:
            a = jnp.exp2(log_beta - inner + tail)
            tail = tail - jnp.sum(sp, axis=-1, keepdims=True)
        pv = jnp.dot(a.astype(BF16), v, preferred_element_type=F32)
        return pv, tail

    for sb in range(n_sub):
        rows = slice(sb * tb, (sb + 1) * tb)
        qb = i * n_sub + sb
        q = q_ref[rows, :]
        pv_d, tail = key_block(q, qb, None, diagonal=True)
        tail = tail - jnp.where(qb > 0, 0.0, 1e30)
        pv_p, tail = key_block(q, jnp.maximum(qb - 1, 0), tail, diagonal=False)
        acc_ref[rows, :] = pv_d + pv_p
        tail_ref[rows, :] = tail

    def more(state):
        step, live = state
        return jnp.logical_and((i + 1) * n_sub - 3 - step >= 0, live)

    def walk(state):
        step, _ = state
        for sb in range(n_sub):
            rows = slice(sb * tb, (sb + 1) * tb)
            j = i * n_sub + sb - 2 - step

            @pl.when(j >= 0)
            def _():
                pv, tail = key_block(q_ref[rows, :], j, tail_ref[rows, :], diagonal=False)
                acc_ref[rows, :] += pv
                tail_ref[rows, :] = tail
        return step + 1, jnp.max(tail_ref[...]) >= F32_EXP2_UNDERFLOW

    lax.while_loop(more, walk, (0, jnp.max(tail_ref[...]) >= F32_EXP2_UNDERFLOW))

    zg = zg_ref[...].astype(F32)
    o_ref[...] = (acc_ref[...] * (zg * jax.nn.sigmoid(zg))).astype(o_ref.dtype)


def _sb_attn(proj, *, n_heads, tb=256, n_sub=8):
    s = proj.shape[0]
    tq = tb * n_sub
    est = 2 * 2 * s * HEAD_DIM * 2 + 6 * tq * HEAD_DIM * 2 + n_sub * 16 * tb * tb * 4
    qspec = lambda off: pl.BlockSpec((tq, HEAD_DIM), lambda h, i: (i, off + h))
    kvspec = lambda off: pl.BlockSpec((s, HEAD_DIM), lambda h, i: (0, off + h))
    return pl.pallas_call(
        functools.partial(_attn_kernel, tb=tb, n_sub=n_sub),
        out_shape=jax.ShapeDtypeStruct((s, n_heads * HEAD_DIM), BF16),
        grid=(n_heads, s // tq),
        in_specs=[qspec(0), kvspec(n_heads), kvspec(2 * n_heads), qspec(3 * n_heads)],
        out_specs=pl.BlockSpec((tq, HEAD_DIM), lambda h, i: (i, h)),
        scratch_shapes=[pltpu.VMEM((tq, HEAD_DIM), F32), pltpu.VMEM((tq, 1), F32)],
        compiler_params=pltpu.CompilerParams(
            dimension_semantics=("arbitrary", "arbitrary"),
            vmem_limit_bytes=_vmem_limit(est)),
        name="sb_attn",
    )(proj, proj, proj, proj)


def _conv_kernel(u_ref, g_ref, zc_ref, up_ref, gp_ref, cw_ref, cb_ref, lng_ref, lnb_ref,
                 w2_ref, b2_ref, o_ref, hbuf, cbuf, *, tm):
    i = pl.program_id(0)
    c = u_ref.shape[1]
    hbuf[CONV_HALO:, :] = u_ref[...].astype(F32) * jax.nn.sigmoid(g_ref[...].astype(F32))
    h_prev = up_ref[...].astype(F32) * jax.nn.sigmoid(gp_ref[...].astype(F32))
    hbuf[:CONV_HALO, :] = jnp.where(i > 0, h_prev, 0.0)

    first = CONV_HALO - (CONV_KERNEL - 1)
    n_win = CONV_ROW_CHUNK + CONV_HALO

    def row_chunk(rc, _):
        r0 = pl.multiple_of(rc * CONV_ROW_CHUNK, CONV_ROW_CHUNK)
        for cc in range(c // V7X_LANES):
            cols = slice(cc * V7X_LANES, (cc + 1) * V7X_LANES)
            window = hbuf[pl.ds(r0, n_win), cols]
            acc = jnp.broadcast_to(cb_ref[:, cols], (CONV_ROW_CHUNK, V7X_LANES))
            for sub in range(8):
                shifted = window if sub == 0 else pltpu.roll(window, n_win - sub, axis=0)
                for base in range(0, CONV_HALO + 1, 8):
                    k = base + sub - first
                    if 0 <= k < CONV_KERNEL:
                        acc = acc + cw_ref[k:k + 1, cols] * shifted[base:base + CONV_ROW_CHUNK, :]
            cbuf[pl.ds(r0, CONV_ROW_CHUNK), cols] = acc
        return 0

    lax.fori_loop(0, tm // CONV_ROW_CHUNK, row_chunk, 0)

    hc = cbuf[...]
    mu = jnp.mean(hc, axis=-1, keepdims=True)
    d = hc - mu
    var = jnp.mean(d * d, axis=-1, keepdims=True)
    y = d * lax.rsqrt(var + LN_EPS) * lng_ref[...] + lnb_ref[...]
    y = y * jax.nn.sigmoid(y)
    p = jnp.dot(y.astype(BF16), w2_ref[...], preferred_element_type=F32) + b2_ref[...]
    zc = zc_ref[...].astype(F32)
    o_ref[...] = (p * (zc * jax.nn.sigmoid(zc))).astype(o_ref.dtype)


def _conv_glu(proj, conv_w, conv_b, ln_g, ln_b, w_pw2, b_pw2, *, u_blk, tm=512):
    s = proj.shape[0]
    c = conv_w.shape[1]
    halo_blocks = tm // CONV_HALO
    est = (2 * 3 * tm * c * 2 + (tm + CONV_HALO) * c * 4 + tm * c * 4 + 2 * c * c * 2
           + 2 * tm * c * 2 + 4 * tm * c * 4)
    cur = lambda off: pl.BlockSpec((tm, c), lambda i: (i, off))
    prev = lambda off: pl.BlockSpec(
        (CONV_HALO, c), lambda i: (jnp.maximum(i * halo_blocks - 1, 0), off))
    row = lambda n: pl.BlockSpec((n, c), lambda i: (0, 0))
    return pl.pallas_call(
        functools.partial(_conv_kernel, tm=tm),
        out_shape=jax.ShapeDtypeStruct((s, c), BF16),
        grid=(s // tm,),
        in_specs=[cur(u_blk), cur(u_blk + 1), cur(u_blk + 2), prev(u_blk), prev(u_blk + 1),
                  row(CONV_KERNEL), row(1), row(1), row(1),
                  pl.BlockSpec((c, c), lambda i: (0, 0)), row(1)],
        out_specs=pl.BlockSpec((tm, c), lambda i: (i, 0)),
        scratch_shapes=[pltpu.VMEM((tm + CONV_HALO, c), F32), pltpu.VMEM((tm, c), F32)],
        compiler_params=pltpu.CompilerParams(
            dimension_semantics=("arbitrary",),
            vmem_limit_bytes=_vmem_limit(est)),
        name="conv_glu",
    )(proj, proj, proj, proj, proj, conv_w, conv_b, ln_g, ln_b, w_pw2, b_pw2)


def _out_proj_kernel(ya_ref, yc_ref, wa_ref, wc_ref, x_ref, g_ref, o_ref):
    y = (jnp.dot(ya_ref[...], wa_ref[...], preferred_element_type=F32)
         + jnp.dot(yc_ref[...], wc_ref[...], preferred_element_type=F32))
    ms = jnp.mean(y * y, axis=-1, keepdims=True)
    o_ref[...] = x_ref[...] + y * lax.rsqrt(ms + RMS_EPS) * g_ref[...]


def _out_proj(y_att, y_conv, w_out, x, g_post, *, tm=512):
    s, d = x.shape
    ca, cc = y_att.shape[1], y_conv.shape[1]
    est = 2 * (ca + cc) * d * 2 + 2 * tm * (ca + cc) * 2 + 4 * tm * d * 4 + 2 * tm * d * 4
    return pl.pallas_call(
        _out_proj_kernel,
        out_shape=jax.ShapeDtypeStruct((s, d), x.dtype),
        grid=(s // tm,),
        in_specs=[
            pl.BlockSpec((tm, ca), lambda i: (i, 0)),
            pl.BlockSpec((tm, cc), lambda i: (i, 0)),
            pl.BlockSpec((ca, d), lambda i: (0, 0)),
            pl.BlockSpec((cc, d), lambda i: (ca // cc, 0)),
            pl.BlockSpec((tm, d), lambda i: (i, 0)),
            pl.BlockSpec((1, d), lambda i: (0, 0)),
        ],
        out_specs=pl.BlockSpec((tm, d), lambda i: (i, 0)),
        compiler_params=pltpu.CompilerParams(
            dimension_semantics=("arbitrary",),
            vmem_limit_bytes=_vmem_limit(est)),
        name="out_proj",
    )(y_att, y_conv, w_out, w_out, x, g_post)


def kernel(x, g_pre, w_in, conv_w, conv_b, ln_g, ln_b, w_pw2, b_pw2, w_out, g_post):
    b, s, d = x.shape
    assert b == 1, "rows of different batch entries must not share a causal history"
    c = conv_w.shape[-1]
    aw = w_out.shape[0] - c
    n_heads = aw // HEAD_DIM
    assert w_in.shape[1] == 4 * aw + 3 * c and aw == c

    col_scale = jnp.concatenate(
        [jnp.full((aw,), LOG2_E / math.sqrt(HEAD_DIM), F32), jnp.ones((w_in.shape[1] - aw,), F32)])

    x2 = x.reshape(s, d)
    proj = _in_proj(x2, g_pre.reshape(1, d), w_in.astype(BF16), col_scale.reshape(1, -1))
    y_att = _sb_attn(proj, n_heads=n_heads)
    y_conv = _conv_glu(proj, conv_w.reshape(CONV_KERNEL, c), conv_b.reshape(1, c),
                       ln_g.reshape(1, c), ln_b.reshape(1, c), w_pw2.astype(BF16),
                       b_pw2.reshape(1, c), u_blk=(4 * aw) // c)
    out = _out_proj(y_att, y_conv, w_out.astype(BF16), x2, g_post.reshape(1, d))
    return out.reshape(b, s, d)
```

```python
import functools
import math

import jax
import jax.numpy as jnp
from jax import lax
from jax.experimental import pallas as pl
from jax.experimental.pallas import tpu as pltpu

F32 = jnp.float32
BF16 = jnp.bfloat16

HEAD_DIM = 128
CONV_KERNEL = 31
RMS_EPS = 1e-6
LN_EPS = 1e-5

V7X_LANES = 128
V7X_VMEM_BYTES = 64 * 1024 * 1024

CONV_HALO = 32
CONV_ROW_CHUNK = 128


def _vmem_limit(nbytes):
    return int(min(nbytes * 3 // 2 + (4 << 20), V7X_VMEM_BYTES - (6 << 20)))


def _in_proj_kernel(xa_ref, xb_ref, g_ref, w_ref, cs_ref, o_ref, h_scr):
    @pl.when(pl.program_id(1) == 0)
    def _():
        half = xa_ref.shape[0]
        for x_ref, rows in ((xa_ref, slice(0, half)), (xb_ref, slice(half, 2 * half))):
            x = x_ref[...]
            ms = jnp.mean(x * x, axis=-1, keepdims=True)
            h_scr[rows, :] = (x * lax.rsqrt(ms + RMS_EPS) * g_ref[...]).astype(BF16)

    acc = (jnp.dot(h_scr[...], w_ref[...], preferred_element_type=F32) * cs_ref[...]).astype(o_ref.dtype)
    for grp in range(o_ref.shape[0]):
        o_ref[grp] = acc[:, grp * V7X_LANES:(grp + 1) * V7X_LANES]


def _in_proj(x, g_pre, w_in, col_scale, *, tm=1024, tn=1024):
    s, d = x.shape
    d_in = w_in.shape[1]
    n_i, n_j = s // tm, d_in // tn
    grp_per_tile = tn // V7X_LANES
    est = 2 * tm * d * 4 + tm * d * 2 + 2 * d * tn * 2 + 2 * tm * tn * 2

    def x_half(which):
        lead = 2 - which
        return pl.BlockSpec(
            (tm // 2, d),
            lambda i, j: (2 * jnp.minimum(i + (j >= n_j - lead).astype(jnp.int32), n_i - 1) + which, 0))

    return pl.pallas_call(
        _in_proj_kernel,
        out_shape=jax.ShapeDtypeStruct((d_in // V7X_LANES, s, V7X_LANES), BF16),
        grid=(n_i, n_j),
        in_specs=[
            x_half(0),
            x_half(1),
            pl.BlockSpec((1, d), lambda i, j: (0, 0)),
            pl.BlockSpec((d, tn), lambda i, j: (0, j)),
            pl.BlockSpec((1, tn), lambda i, j: (0, j)),
        ],
        out_specs=pl.BlockSpec((grp_per_tile, tm, V7X_LANES), lambda i, j: (j, i, 0)),
        scratch_shapes=[pltpu.VMEM((tm, d), BF16)],
        compiler_params=pltpu.CompilerParams(
            dimension_semantics=("arbitrary", "arbitrary"),
            vmem_limit_bytes=_vmem_limit(est)),
        name="in_proj",
    )(x, x, g_pre, w_in, col_scale)


LOG2_E = math.log2(math.e)
F32_EXP2_UNDERFLOW = -151.5


def _softplus2(z2):
    return jnp.maximum(z2, 0.0) + jnp.log(1.0 + jnp.exp2(-jnp.abs(z2))) * LOG2_E


def _attn_kernel(q_ref, k_ref, v_ref, zg_ref, o_ref, acc_ref, tail_ref, *, tb, n_sub):
    i = pl.program_id(1)
    row = lax.broadcasted_iota(jnp.int32, (tb, tb), 0)
    col = lax.broadcasted_iota(jnp.int32, (tb, tb), 1)
    later = (row > col).astype(BF16)
    causal = col < row

    def key_block(q, j, tail, diagonal):
        ks = pl.multiple_of(j * tb, tb)
        k = k_ref[pl.ds(ks, tb), :]
        v = v_ref[pl.ds(ks, tb), :]
        z2 = lax.dot_general(q, k, (((1,), (1,)), ((), ())), preferred_element_type=F32)
        sp = _softplus2(z2)
        log_beta = z2 - sp
        if diagonal:
            sp = jnp.where(causal, sp, 0.0)
        sp_hi = sp.astype(BF16)
        sp_lo = (sp - sp_hi.astype(F32)).astype(BF16)
        inner = (jnp.dot(sp_hi, later, preferred_element_type=F32)
                 + jnp.dot(sp_lo, later, preferred_element_type=F32))
        if diagonal:
            a = jnp.where(causal, jnp.exp2(log_beta - inner), 0.0)
            tail = -jnp.sum(sp, axis=-1, keepdims=True)
        else:
            a = jnp.exp2(log_beta - inner + tail)
            tail = tail - jnp.sum(sp, axis=-1, keepdims=True)
        pv = jnp.dot(a.astype(BF16), v, preferred_element_type=F32)
        return pv, tail

    for sb in range(n_sub):
        rows = slice(sb * tb, (sb + 1) * tb)
        qb = i * n_sub + sb
        q = q_ref[rows, :]
        pv_d, tail = key_block(q, qb, None, diagonal=True)
        tail = tail - jnp.where(qb > 0, 0.0, 1e30)
        pv_p, tail = key_block(q, jnp.maximum(qb - 1, 0), tail, diagonal=False)
        acc_ref[rows, :] = pv_d + pv_p
        tail_ref[rows, :] = tail

    def more(state):
        step, live = state
        return jnp.logical_and((i + 1) * n_sub - 3 - step >= 0, live)

    def walk(state):
        step, _ = state
        for sb in range(n_sub):
            rows = slice(sb * tb, (sb + 1) * tb)
            j = i * n_sub + sb - 2 - step

            @pl.when(j >= 0)
            def _():
                pv, tail = key_block(q_ref[rows, :], j, tail_ref[rows, :], diagonal=False)
                acc_ref[rows, :] += pv
                tail_ref[rows, :] = tail
        return step + 1, jnp.max(tail_ref[...]) >= F32_EXP2_UNDERFLOW

    lax.while_loop(more, walk, (0, jnp.max(tail_ref[...]) >= F32_EXP2_UNDERFLOW))

    zg = zg_ref[...].astype(F32)
    o_ref[...] = (acc_ref[...] * (zg * jax.nn.sigmoid(zg))).astype(o_ref.dtype)


def _sb_attn(proj, *, n_heads, tb=256, n_sub=8):
    s = proj.shape[1]
    tq = tb * n_sub
    est = 2 * 2 * s * HEAD_DIM * 2 + 6 * tq * HEAD_DIM * 2 + n_sub * 16 * tb * tb * 4
    qspec = lambda off: pl.BlockSpec((None, tq, HEAD_DIM), lambda h, i: (off + h, i, 0))
    kvspec = lambda off: pl.BlockSpec((None, s, HEAD_DIM), lambda h, i: (off + h, 0, 0))
    return pl.pallas_call(
        functools.partial(_attn_kernel, tb=tb, n_sub=n_sub),
        out_shape=jax.ShapeDtypeStruct((s, n_heads * HEAD_DIM), BF16),
        grid=(n_heads, s // tq),
        in_specs=[qspec(0), kvspec(n_heads), kvspec(2 * n_heads), qspec(3 * n_heads)],
        out_specs=pl.BlockSpec((tq, HEAD_DIM), lambda h, i: (i, h)),
        scratch_shapes=[pltpu.VMEM((tq, HEAD_DIM), F32), pltpu.VMEM((tq, 1), F32)],
        compiler_params=pltpu.CompilerParams(
            dimension_semantics=("arbitrary", "arbitrary"),
            vmem_limit_bytes=_vmem_limit(est)),
        name="sb_attn",
    )(proj, proj, proj, proj)


def _conv_kernel(u_ref, g_ref, zc_ref, up_ref, gp_ref, cw_ref, cb_ref, lng_ref, lnb_ref,
                 w2_ref, b2_ref, o_ref, hbuf, cbuf, *, tm):
    i = pl.program_id(0)
    n_grp = u_ref.shape[0]
    c = n_grp * V7X_LANES
    for grp in range(n_grp):
        cols = slice(grp * V7X_LANES, (grp + 1) * V7X_LANES)
        hbuf[CONV_HALO:, cols] = u_ref[grp].astype(F32) * jax.nn.sigmoid(g_ref[grp].astype(F32))
        h_prev = up_ref[grp].astype(F32) * jax.nn.sigmoid(gp_ref[grp].astype(F32))
        hbuf[:CONV_HALO, cols] = jnp.where(i > 0, h_prev, 0.0)

    first = CONV_HALO - (CONV_KERNEL - 1)
    n_win = CONV_ROW_CHUNK + CONV_HALO

    def row_chunk(rc, _):
        r0 = pl.multiple_of(rc * CONV_ROW_CHUNK, CONV_ROW_CHUNK)
        for cc in range(c // V7X_LANES):
            cols = slice(cc * V7X_LANES, (cc + 1) * V7X_LANES)
            window = hbuf[pl.ds(r0, n_win), cols]
            acc = jnp.broadcast_to(cb_ref[:, cols], (CONV_ROW_CHUNK, V7X_LANES))
            for sub in range(8):
                shifted = window if sub == 0 else pltpu.roll(window, n_win - sub, axis=0)
                for base in range(0, CONV_HALO + 1, 8):
                    k = base + sub - first
                    if 0 <= k < CONV_KERNEL:
                        acc = acc + cw_ref[k:k + 1, cols] * shifted[base:base + CONV_ROW_CHUNK, :]
            cbuf[pl.ds(r0, CONV_ROW_CHUNK), cols] = acc
        return 0

    lax.fori_loop(0, tm // CONV_ROW_CHUNK, row_chunk, 0)

    hc = cbuf[...]
    mu = jnp.mean(hc, axis=-1, keepdims=True)
    d = hc - mu
    var = jnp.mean(d * d, axis=-1, keepdims=True)
    y = d * lax.rsqrt(var + LN_EPS) * lng_ref[...] + lnb_ref[...]
    y = y * jax.nn.sigmoid(y)
    p = jnp.dot(y.astype(BF16), w2_ref[...], preferred_element_type=F32) + b2_ref[...]
    for grp in range(n_grp):
        cols = slice(grp * V7X_LANES, (grp + 1) * V7X_LANES)
        zc = zc_ref[grp].astype(F32)
        o_ref[:, cols] = (p[:, cols] * (zc * jax.nn.sigmoid(zc))).astype(o_ref.dtype)


def _conv_glu(proj, conv_w, conv_b, ln_g, ln_b, w_pw2, b_pw2, *, u_blk, tm=512):
    s = proj.shape[1]
    c = conv_w.shape[1]
    n_grp = c // V7X_LANES
    halo_blocks = tm // CONV_HALO
    est = (2 * 3 * tm * c * 2 + (tm + CONV_HALO) * c * 4 + tm * c * 4 + 2 * c * c * 2
           + 2 * tm * c * 2 + 4 * tm * c * 4)
    cur = lambda off: pl.BlockSpec((n_grp, tm, V7X_LANES), lambda i: (off, i, 0))
    prev = lambda off: pl.BlockSpec(
        (n_grp, CONV_HALO, V7X_LANES), lambda i: (off, jnp.maximum(i * halo_blocks - 1, 0), 0))
    row = lambda n: pl.BlockSpec((n, c), lambda i: (0, 0))
    return pl.pallas_call(
        functools.partial(_conv_kernel, tm=tm),
        out_shape=jax.ShapeDtypeStruct((s, c), BF16),
        grid=(s // tm,),
        in_specs=[cur(u_blk), cur(u_blk + 1), cur(u_blk + 2), prev(u_blk), prev(u_blk + 1),
                  row(CONV_KERNEL), row(1), row(1), row(1),
                  pl.BlockSpec((c, c), lambda i: (0, 0)), row(1)],
        out_specs=pl.BlockSpec((tm, c), lambda i: (i, 0)),
        scratch_shapes=[pltpu.VMEM((tm + CONV_HALO, c), F32), pltpu.VMEM((tm, c), F32)],
        compiler_params=pltpu.CompilerParams(
            dimension_semantics=("arbitrary",),
            vmem_limit_bytes=_vmem_limit(est)),
        name="conv_glu",
    )(proj, proj, proj, proj, proj, conv_w, conv_b, ln_g, ln_b, w_pw2, b_pw2)


def _out_proj_kernel(ya_ref, yc_ref, wa_ref, wc_ref, x_ref, g_ref, o_ref):
    y = (jnp.dot(ya_ref[...], wa_ref[...], preferred_element_type=F32)
         + jnp.dot(yc_ref[...], wc_ref[...], preferred_element_type=F32))
    ms = jnp.mean(y * y, axis=-1, keepdims=True)
    o_ref[...] = x_ref[...] + y * lax.rsqrt(ms + RMS_EPS) * g_ref[...]


def _out_proj(y_att, y_conv, w_out, x, g_post, *, tm=512):
    s, d = x.shape
    ca, cc = y_att.shape[1], y_conv.shape[1]
    est = 2 * (ca + cc) * d * 2 + 2 * tm * (ca + cc) * 2 + 4 * tm * d * 4 + 2 * tm * d * 4
    return pl.pallas_call(
        _out_proj_kernel,
        out_shape=jax.ShapeDtypeStruct((s, d), x.dtype),
        grid=(s // tm,),
        in_specs=[
            pl.BlockSpec((tm, ca), lambda i: (i, 0)),
            pl.BlockSpec((tm, cc), lambda i: (i, 0)),
            pl.BlockSpec((ca, d), lambda i: (0, 0)),
            pl.BlockSpec((cc, d), lambda i: (ca // cc, 0)),
            pl.BlockSpec((tm, d), lambda i: (i, 0)),
            pl.BlockSpec((1, d), lambda i: (0, 0)),
        ],
        out_specs=pl.BlockSpec((tm, d), lambda i: (i, 0)),
        compiler_params=pltpu.CompilerParams(
            dimension_semantics=("arbitrary",),
            vmem_limit_bytes=_vmem_limit(est)),
        name="out_proj",
    )(y_att, y_conv, w_out, w_out, x, g_post)


def kernel(x, g_pre, w_in, conv_w, conv_b, ln_g, ln_b, w_pw2, b_pw2, w_out, g_post):
    b, s, d = x.shape
    assert b == 1, "rows of different batch entries must not share a causal history"
    c = conv_w.shape[-1]
    aw = w_out.shape[0] - c
    n_heads = aw // HEAD_DIM
    assert w_in.shape[1] == 4 * aw + 3 * c and aw == c

    col_scale = jnp.concatenate(
        [jnp.full((aw,), LOG2_E / math.sqrt(HEAD_DIM), F32), jnp.ones((w_in.shape[1] - aw,), F32)])

    x2 = x.reshape(s, d)
    proj = _in_proj(x2, g_pre.reshape(1, d), w_in.astype(BF16), col_scale.reshape(1, -1))
    y_att = _sb_attn(proj, n_heads=n_heads)
    y_conv = _conv_glu(proj, conv_w.reshape(CONV_KERNEL, c), conv_b.reshape(1, c),
                       ln_g.reshape(1, c), ln_b.reshape(1, c), w_pw2.astype(BF16),
                       b_pw2.reshape(1, c), u_blk=(4 * aw) // c)
    out = _out_proj(y_att, y_conv, w_out.astype(BF16), x2, g_post.reshape(1, d))
    return out.reshape(b, s, d)
```

```python
import functools
import math

import jax
import jax.numpy as jnp
from jax import lax
from jax.experimental import pallas as pl
from jax.experimental.pallas import tpu as pltpu

F32 = jnp.float32
BF16 = jnp.bfloat16

HEAD_DIM = 128
CONV_KERNEL = 31
RMS_EPS = 1e-6
LN_EPS = 1e-5

V7X_LANES = 128
V7X_VMEM_BYTES = 64 * 1024 * 1024

CONV_HALO = 32
CONV_ROW_CHUNK = 128


def _vmem_limit(nbytes):
    return int(min(nbytes * 3 // 2 + (4 << 20), V7X_VMEM_BYTES - (6 << 20)))


def _in_proj_kernel(xa_ref, xb_ref, g_ref, w_ref, cs_ref, o_ref, h_scr):
    @pl.when(pl.program_id(1) == 0)
    def _():
        half = xa_ref.shape[0]
        for x_ref, rows in ((xa_ref, slice(0, half)), (xb_ref, slice(half, 2 * half))):
            x = x_ref[...]
            ms = jnp.mean(x * x, axis=-1, keepdims=True)
            h_scr[rows, :] = (x * lax.rsqrt(ms + RMS_EPS) * g_ref[...]).astype(BF16)

    acc = (jnp.dot(h_scr[...], w_ref[...].astype(BF16), preferred_element_type=F32)
           * cs_ref[...]).astype(o_ref.dtype)
    for grp in range(o_ref.shape[0]):
        o_ref[grp] = acc[:, grp * V7X_LANES:(grp + 1) * V7X_LANES]


def _in_proj(x, g_pre, w_in, col_scale, *, tm=1024, tn=1024):
    s, d = x.shape
    d_in = w_in.shape[1]
    n_i, n_j = s // tm, d_in // tn
    grp_per_tile = tn // V7X_LANES
    est = 2 * tm * d * 4 + tm * d * 2 + 2 * d * tn * 4 + 2 * tm * tn * 2

    def x_half(which):
        lead = 2 - which
        return pl.BlockSpec(
            (tm // 2, d),
            lambda i, j: (2 * jnp.minimum(i + (j >= n_j - lead).astype(jnp.int32), n_i - 1) + which, 0))

    return pl.pallas_call(
        _in_proj_kernel,
        out_shape=jax.ShapeDtypeStruct((d_in // V7X_LANES, s, V7X_LANES), BF16),
        grid=(n_i, n_j),
        in_specs=[
            x_half(0),
            x_half(1),
            pl.BlockSpec((1, d), lambda i, j: (0, 0)),
            pl.BlockSpec((d, tn), lambda i, j: (0, j)),
            pl.BlockSpec((1, tn), lambda i, j: (0, j)),
        ],
        out_specs=pl.BlockSpec((grp_per_tile, tm, V7X_LANES), lambda i, j: (j, i, 0)),
        scratch_shapes=[pltpu.VMEM((tm, d), BF16)],
        compiler_params=pltpu.CompilerParams(
            dimension_semantics=("arbitrary", "arbitrary"),
            vmem_limit_bytes=_vmem_limit(est)),
        name="in_proj",
    )(x, x, g_pre, w_in, col_scale)


LOG2_E = math.log2(math.e)
F32_EXP2_UNDERFLOW = -151.5


def _softplus2(z2):
    return jnp.maximum(z2, 0.0) + jnp.log(1.0 + jnp.exp2(-jnp.abs(z2))) * LOG2_E


def _attn_kernel(q_ref, k_ref, v_ref, zg_ref, o_ref, acc_ref, tail_ref, *, tb, n_sub):
    i = pl.program_id(1)
    row = lax.broadcasted_iota(jnp.int32, (tb, tb), 0)
    col = lax.broadcasted_iota(jnp.int32, (tb, tb), 1)
    later = (row > col).astype(BF16)
    causal = col < row

    def key_block(q, j, tail, diagonal):
        ks = pl.multiple_of(j * tb, tb)
        k = k_ref[pl.ds(ks, tb), :]
        v = v_ref[pl.ds(ks, tb), :]
        z2 = lax.dot_general(q, k, (((1,), (1,)), ((), ())), preferred_element_type=F32)
        sp = _softplus2(z2)
        log_beta = z2 - sp
        if diagonal:
            sp = jnp.where(causal, sp, 0.0)
        sp_hi = sp.astype(BF16)
        sp_lo = (sp - sp_hi.astype(F32)).astype(BF16)
        inner = (jnp.dot(sp_hi, later, preferred_element_type=F32)
                 + jnp.dot(sp_lo, later, preferred_element_type=F32))
        if diagonal:
            a = jnp.where(causal, jnp.exp2(log_beta - inner), 0.0)
            tail = -jnp.sum(sp, axis=-1, keepdims=True)
        else:
            a = jnp.exp2(log_beta - inner + tail)
            tail = tail - jnp.sum(sp, axis=-1, keepdims=True)
        pv = jnp.dot(a.astype(BF16), v, preferred_element_type=F32)
        return pv, tail

    for sb in range(n_sub):
        rows = slice(sb * tb, (sb + 1) * tb)
        qb = i * n_sub + sb
        q = q_ref[rows, :]
        pv_d, tail = key_block(q, qb, None, diagonal=True)
        tail = tail - jnp.where(qb > 0, 0.0, 1e30)
        pv_p, tail = key_block(q, jnp.maximum(qb - 1, 0), tail, diagonal=False)
        acc_ref[rows, :] = pv_d + pv_p
        tail_ref[rows, :] = tail

    def more(state):
        step, live = state
        return jnp.logical_and((i + 1) * n_sub - 3 - step >= 0, live)

    def walk(state):
        step, _ = state
        for sb in range(n_sub):
            rows = slice(sb * tb, (sb + 1) * tb)
            j = i * n_sub + sb - 2 - step

            @pl.when(j >= 0)
            def _():
                pv, tail = key_block(q_ref[rows, :], j, tail_ref[rows, :], diagonal=False)
                acc_ref[rows, :] += pv
                tail_ref[rows, :] = tail
        return step + 1, jnp.max(tail_ref[...]) >= F32_EXP2_UNDERFLOW

    lax.while_loop(more, walk, (0, jnp.max(tail_ref[...]) >= F32_EXP2_UNDERFLOW))

    zg = zg_ref[...].astype(F32)
    o_ref[...] = (acc_ref[...] * (zg * jax.nn.sigmoid(zg))).astype(o_ref.dtype)


def _sb_attn(proj, *, n_heads, tb=256, n_sub=16):
    s = proj.shape[1]
    tq = tb * n_sub
    est = 2 * 2 * s * HEAD_DIM * 2 + 6 * tq * HEAD_DIM * 2 + n_sub * 16 * tb * tb * 4
    qspec = lambda off: pl.BlockSpec((None, tq, HEAD_DIM), lambda h, i: (off + h, i, 0))
    kvspec = lambda off: pl.BlockSpec((None, s, HEAD_DIM), lambda h, i: (off + h, 0, 0))
    return pl.pallas_call(
        functools.partial(_attn_kernel, tb=tb, n_sub=n_sub),
        out_shape=jax.ShapeDtypeStruct((s, n_heads * HEAD_DIM), BF16),
        grid=(n_heads, s // tq),
        in_specs=[qspec(0), kvspec(n_heads), kvspec(2 * n_heads), qspec(3 * n_heads)],
        out_specs=pl.BlockSpec((tq, HEAD_DIM), lambda h, i: (i, h)),
        scratch_shapes=[pltpu.VMEM((tq, HEAD_DIM), F32), pltpu.VMEM((tq, 1), F32)],
        compiler_params=pltpu.CompilerParams(
            dimension_semantics=("arbitrary", "arbitrary"),
            vmem_limit_bytes=_vmem_limit(est)),
        name="sb_attn",
    )(proj, proj, proj, proj)


def _conv_kernel(u_ref, g_ref, zc_ref, up_ref, gp_ref, cw_ref, cb_ref, lng_ref, lnb_ref,
                 w2_ref, b2_ref, o_ref, hbuf, cbuf, *, tm):
    i = pl.program_id(0)
    n_grp = u_ref.shape[0]
    c = n_grp * V7X_LANES
    for grp in range(n_grp):
        cols = slice(grp * V7X_LANES, (grp + 1) * V7X_LANES)
        hbuf[CONV_HALO:, cols] = u_ref[grp].astype(F32) * jax.nn.sigmoid(g_ref[grp].astype(F32))
        h_prev = up_ref[grp].astype(F32) * jax.nn.sigmoid(gp_ref[grp].astype(F32))
        hbuf[:CONV_HALO, cols] = jnp.where(i > 0, h_prev, 0.0)

    first = CONV_HALO - (CONV_KERNEL - 1)
    n_win = CONV_ROW_CHUNK + CONV_HALO

    def row_chunk(rc, _):
        r0 = pl.multiple_of(rc * CONV_ROW_CHUNK, CONV_ROW_CHUNK)
        for cc in range(c // V7X_LANES):
            cols = slice(cc * V7X_LANES, (cc + 1) * V7X_LANES)
            window = hbuf[pl.ds(r0, n_win), cols]
            acc = jnp.broadcast_to(cb_ref[:, cols], (CONV_ROW_CHUNK, V7X_LANES))
            for sub in range(8):
                shifted = window if sub == 0 else pltpu.roll(window, n_win - sub, axis=0)
                for base in range(0, CONV_HALO + 1, 8):
                    k = base + sub - first
                    if 0 <= k < CONV_KERNEL:
                        acc = acc + cw_ref[k:k + 1, cols] * shifted[base:base + CONV_ROW_CHUNK, :]
            cbuf[pl.ds(r0, CONV_ROW_CHUNK), cols] = acc
        return 0

    lax.fori_loop(0, tm // CONV_ROW_CHUNK, row_chunk, 0)

    hc = cbuf[...]
    mu = jnp.mean(hc, axis=-1, keepdims=True)
    d = hc - mu
    var = jnp.mean(d * d, axis=-1, keepdims=True)
    y = d * lax.rsqrt(var + LN_EPS) * lng_ref[...] + lnb_ref[...]
    y = y * jax.nn.sigmoid(y)
    p = jnp.dot(y.astype(BF16), w2_ref[...], preferred_element_type=F32) + b2_ref[...]
    for grp in range(n_grp):
        cols = slice(grp * V7X_LANES, (grp + 1) * V7X_LANES)
        zc = zc_ref[grp].astype(F32)
        o_ref[:, cols] = (p[:, cols] * (zc * jax.nn.sigmoid(zc))).astype(o_ref.dtype)


def _conv_glu(proj, conv_w, conv_b, ln_g, ln_b, w_pw2, b_pw2, *, u_blk, tm=512):
    s = proj.shape[1]
    c = conv_w.shape[1]
    n_grp = c // V7X_LANES
    halo_blocks = tm // CONV_HALO
    est = (2 * 3 * tm * c * 2 + (tm + CONV_HALO) * c * 4 + tm * c * 4 + 2 * c * c * 2
           + 2 * tm * c * 2 + 4 * tm * c * 4)
    cur = lambda off: pl.BlockSpec((n_grp, tm, V7X_LANES), lambda i: (off, i, 0))
    prev = lambda off: pl.BlockSpec(
        (n_grp, CONV_HALO, V7X_LANES), lambda i: (off, jnp.maximum(i * halo_blocks - 1, 0), 0))
    row = lambda n: pl.BlockSpec((n, c), lambda i: (0, 0))
    return pl.pallas_call(
        functools.partial(_conv_kernel, tm=tm),
        out_shape=jax.ShapeDtypeStruct((s, c), BF16),
        grid=(s // tm,),
        in_specs=[cur(u_blk), cur(u_blk + 1), cur(u_blk + 2), prev(u_blk), prev(u_blk + 1),
                  row(CONV_KERNEL), row(1), row(1), row(1),
                  pl.BlockSpec((c, c), lambda i: (0, 0)), row(1)],
        out_specs=pl.BlockSpec((tm, c), lambda i: (i, 0)),
        scratch_shapes=[pltpu.VMEM((tm + CONV_HALO, c), F32), pltpu.VMEM((tm, c), F32)],
        compiler_params=pltpu.CompilerParams(
            dimension_semantics=("arbitrary",),
            vmem_limit_bytes=_vmem_limit(est)),
        name="conv_glu",
    )(proj, proj, proj, proj, proj, conv_w, conv_b, ln_g, ln_b, w_pw2, b_pw2)


def _out_proj_kernel(ya_ref, yc_ref, w_ref, x_ref, g_ref, o_ref, wb_scr):
    @pl.when(pl.program_id(0) == 0)
    def _():
        wb_scr[...] = w_ref[...].astype(BF16)

    ca = ya_ref.shape[1]
    y = (jnp.dot(ya_ref[...], wb_scr[:ca, :], preferred_element_type=F32)
         + jnp.dot(yc_ref[...], wb_scr[ca:, :], preferred_element_type=F32))
    ms = jnp.mean(y * y, axis=-1, keepdims=True)
    o_ref[...] = x_ref[...] + y * lax.rsqrt(ms + RMS_EPS) * g_ref[...]


def _out_proj(y_att, y_conv, w_out, x, g_post, *, tm=512):
    s, d = x.shape
    ca, cc = y_att.shape[1], y_conv.shape[1]
    est = (ca + cc) * d * (4 + 2) + 2 * tm * (ca + cc) * 2 + 4 * tm * d * 4 + 2 * tm * d * 4
    return pl.pallas_call(
        _out_proj_kernel,
        out_shape=jax.ShapeDtypeStruct((s, d), x.dtype),
        grid=(s // tm,),
        in_specs=[
            pl.BlockSpec((tm, ca), lambda i: (i, 0)),
            pl.BlockSpec((tm, cc), lambda i: (i, 0)),
            pl.BlockSpec((ca + cc, d), lambda i: (0, 0), pipeline_mode=pl.Buffered(1)),
            pl.BlockSpec((tm, d), lambda i: (i, 0)),
            pl.BlockSpec((1, d), lambda i: (0, 0)),
        ],
        out_specs=pl.BlockSpec((tm, d), lambda i: (i, 0)),
        scratch_shapes=[pltpu.VMEM((ca + cc, d), BF16)],
        compiler_params=pltpu.CompilerParams(
            dimension_semantics=("arbitrary",),
            vmem_limit_bytes=_vmem_limit(est)),
        name="out_proj",
    )(y_att, y_conv, w_out, x, g_post)


def kernel(x, g_pre, w_in, conv_w, conv_b, ln_g, ln_b, w_pw2, b_pw2, w_out, g_post):
    b, s, d = x.shape
    assert b == 1, "rows of different batch entries must not share a causal history"
    c = conv_w.shape[-1]
    aw = w_out.shape[0] - c
    n_heads = aw // HEAD_DIM
    assert w_in.shape[1] == 4 * aw + 3 * c and aw == c

    col_scale = jnp.concatenate(
        [jnp.full((aw,), LOG2_E / math.sqrt(HEAD_DIM), F32), jnp.ones((w_in.shape[1] - aw,), F32)])

    x2 = x.reshape(s, d)
    proj = _in_proj(x2, g_pre.reshape(1, d), w_in, col_scale.reshape(1, -1))
    y_att = _sb_attn(proj, n_heads=n_heads)
    y_conv = _conv_glu(proj, conv_w.reshape(CONV_KERNEL, c), conv_b.reshape(1, c),
                       ln_g.reshape(1, c), ln_b.reshape(1, c), w_pw2.astype(BF16),
                       b_pw2.reshape(1, c), u_blk=(4 * aw) // c)
    out = _out_proj(y_att, y_conv, w_out, x2, g_post.reshape(1, d))
    return out.reshape(b, s, d)
```

```python
import functools
import math

import jax
import jax.numpy as jnp
from jax import lax
from jax.experimental import pallas as pl
from jax.experimental.pallas import tpu as pltpu

F32 = jnp.float32
BF16 = jnp.bfloat16

HEAD_DIM = 128
CONV_KERNEL = 31
RMS_EPS = 1e-6
LN_EPS = 1e-5

V7X_LANES = 128
V7X_VMEM_BYTES = 64 * 1024 * 1024

CONV_HALO = 32
CONV_TIME_CHUNK = 32


def _vmem_limit(nbytes):
    return int(min(nbytes * 3 // 2 + (4 << 20), V7X_VMEM_BYTES - (6 << 20)))


X_PARTS = 4


def _in_proj_kernel(*refs):
    x_refs = refs[:X_PARTS]
    g_ref, w_ref, cs_ref, o_ref, h_scr = refs[X_PARTS:]

    @pl.when(pl.program_id(1) == 0)
    def _():
        part = x_refs[0].shape[0]
        for n, x_ref in enumerate(x_refs):
            x = x_ref[...]
            ms = jnp.mean(x * x, axis=-1, keepdims=True)
            h_scr[n * part:(n + 1) * part, :] = (x * lax.rsqrt(ms + RMS_EPS) * g_ref[...]).astype(BF16)

    acc = (jnp.dot(h_scr[...], w_ref[...].astype(BF16), preferred_element_type=F32)
           * cs_ref[...]).astype(o_ref.dtype)
    for grp in range(o_ref.shape[0]):
        o_ref[grp] = acc[:, grp * V7X_LANES:(grp + 1) * V7X_LANES]


def _in_proj(x, g_pre, w_in, col_scale, *, tm=1024, tn=1024):
    s, d = x.shape
    d_in = w_in.shape[1]
    n_i, n_j = s // tm, d_in // tn
    grp_per_tile = tn // V7X_LANES
    est = 2 * tm * d * 4 + tm * d * 2 + 2 * d * tn * 4 + 2 * tm * tn * 2

    assert n_j >= X_PARTS

    def x_part(which):
        lead = X_PARTS - which
        return pl.BlockSpec(
            (tm // X_PARTS, d),
            lambda i, j: (X_PARTS * jnp.minimum(i + (j >= n_j - lead).astype(jnp.int32), n_i - 1)
                          + which, 0))

    return pl.pallas_call(
        _in_proj_kernel,
        out_shape=jax.ShapeDtypeStruct((d_in // V7X_LANES, s, V7X_LANES), BF16),
        grid=(n_i, n_j),
        in_specs=[x_part(n) for n in range(X_PARTS)] + [
            pl.BlockSpec((1, d), lambda i, j: (0, 0)),
            pl.BlockSpec((d, tn), lambda i, j: (0, j)),
            pl.BlockSpec((1, tn), lambda i, j: (0, j)),
        ],
        out_specs=pl.BlockSpec((grp_per_tile, tm, V7X_LANES), lambda i, j: (j, i, 0)),
        scratch_shapes=[pltpu.VMEM((tm, d), BF16)],
        compiler_params=pltpu.CompilerParams(
            dimension_semantics=("arbitrary", "arbitrary"),
            vmem_limit_bytes=_vmem_limit(est)),
        name="in_proj",
    )(*([x] * X_PARTS), g_pre, w_in, col_scale)


LOG2_E = math.log2(math.e)
F32_EXP2_UNDERFLOW = -151.5


def _softplus2(z2):
    return jnp.maximum(z2, 0.0) + jnp.log(1.0 + jnp.exp2(-jnp.abs(z2))) * LOG2_E


def _attn_kernel(q_ref, k_ref, v_ref, zg_ref, o_ref, acc_ref, tail_ref, *, tb, n_sub):
    i = pl.program_id(1)
    row = lax.broadcasted_iota(jnp.int32, (tb, tb), 0)
    col = lax.broadcasted_iota(jnp.int32, (tb, tb), 1)
    later = (row > col).astype(BF16)
    causal = col < row

    def key_block(q, j, tail, diagonal):
        ks = pl.multiple_of(j * tb, tb)
        k = k_ref[pl.ds(ks, tb), :]
        v = v_ref[pl.ds(ks, tb), :]
        z2 = lax.dot_general(q, k, (((1,), (1,)), ((), ())), preferred_element_type=F32)
        sp = _softplus2(z2)
        log_beta = z2 - sp
        if diagonal:
            sp = jnp.where(causal, sp, 0.0)
        sp_hi = sp.astype(BF16)
        sp_lo = (sp - sp_hi.astype(F32)).astype(BF16)
        inner = (jnp.dot(sp_hi, later, preferred_element_type=F32)
                 + jnp.dot(sp_lo, later, preferred_element_type=F32))
        if diagonal:
            a = jnp.where(causal, jnp.exp2(log_beta - inner), 0.0)
            tail = -jnp.sum(sp, axis=-1, keepdims=True)
        else:
            a = jnp.exp2(log_beta - inner + tail)
            tail = tail - jnp.sum(sp, axis=-1, keepdims=True)
        pv = jnp.dot(a.astype(BF16), v, preferred_element_type=F32)
        return pv, tail

    for sb in range(n_sub):
        rows = slice(sb * tb, (sb + 1) * tb)
        qb = i * n_sub + sb
        q = q_ref[rows, :]
        pv_d, tail = key_block(q, qb, None, diagonal=True)
        tail = tail - jnp.where(qb > 0, 0.0, 1e30)
        pv_p, tail = key_block(q, jnp.maximum(qb - 1, 0), tail, diagonal=False)
        acc_ref[rows, :] = pv_d + pv_p
        tail_ref[rows, :] = tail

    def more(state):
        step, live = state
        return jnp.logical_and((i + 1) * n_sub - 3 - step >= 0, live)

    def walk(state):
        step, _ = state
        for sb in range(n_sub):
            rows = slice(sb * tb, (sb + 1) * tb)
            j = i * n_sub + sb - 2 - step

            @pl.when(j >= 0)
            def _():
                pv, tail = key_block(q_ref[rows, :], j, tail_ref[rows, :], diagonal=False)
                acc_ref[rows, :] += pv
                tail_ref[rows, :] = tail
        return step + 1, jnp.max(tail_ref[...]) >= F32_EXP2_UNDERFLOW

    lax.while_loop(more, walk, (0, jnp.max(tail_ref[...]) >= F32_EXP2_UNDERFLOW))

    zg = zg_ref[...].astype(F32)
    o_ref[...] = (acc_ref[...] * (zg * jax.nn.sigmoid(zg))).astype(o_ref.dtype)


def _sb_attn(proj, *, n_heads, tb=256, n_sub=16):
    s = proj.shape[1]
    tq = tb * n_sub
    est = 2 * 2 * s * HEAD_DIM * 2 + 6 * tq * HEAD_DIM * 2 + n_sub * 16 * tb * tb * 4
    qspec = lambda off: pl.BlockSpec((None, tq, HEAD_DIM), lambda h, i: (off + h, i, 0))
    kvspec = lambda off: pl.BlockSpec((None, s, HEAD_DIM), lambda h, i: (off + h, 0, 0))
    return pl.pallas_call(
        functools.partial(_attn_kernel, tb=tb, n_sub=n_sub),
        out_shape=jax.ShapeDtypeStruct((s, n_heads * HEAD_DIM), BF16),
        grid=(n_heads, s // tq),
        in_specs=[qspec(0), kvspec(n_heads), kvspec(2 * n_heads), qspec(3 * n_heads)],
        out_specs=pl.BlockSpec((tq, HEAD_DIM), lambda h, i: (i, h)),
        scratch_shapes=[pltpu.VMEM((tq, HEAD_DIM), F32), pltpu.VMEM((tq, 1), F32)],
        compiler_params=pltpu.CompilerParams(
            dimension_semantics=("arbitrary", "arbitrary"),
            vmem_limit_bytes=_vmem_limit(est)),
        name="sb_attn",
    )(proj, proj, proj, proj)


def _conv_kernel(u_ref, g_ref, zc_ref, up_ref, gp_ref, cw_ref, cb_ref, lng_ref, lnb_ref,
                 w2_ref, b2_ref, o_ref, hbuf, cbuf, *, tm):
    i = pl.program_id(0)
    n_grp = u_ref.shape[0]
    for grp in range(n_grp):
        h_cur = u_ref[grp].astype(F32) * jax.nn.sigmoid(g_ref[grp].astype(F32))
        hbuf[pl.ds(CONV_HALO * n_grp + grp, tm, stride=n_grp), :] = h_cur
        h_prev = up_ref[grp].astype(F32) * jax.nn.sigmoid(gp_ref[grp].astype(F32))
        hbuf[pl.ds(grp, CONV_HALO, stride=n_grp), :] = jnp.where(i > 0, h_prev, 0.0)

    first = CONV_HALO - (CONV_KERNEL - 1)
    chunk_rows = CONV_TIME_CHUNK * n_grp

    def time_chunk(tc, _):
        t0 = tc * CONV_TIME_CHUNK
        acc = jnp.tile(cb_ref[...], (CONV_TIME_CHUNK, 1))
        for k in range(CONV_KERNEL):
            start = pl.multiple_of((t0 + first + k) * n_grp, n_grp)
            tap = jnp.tile(cw_ref[k * n_grp:(k + 1) * n_grp, :], (CONV_TIME_CHUNK, 1))
            acc = acc + tap * hbuf[pl.ds(start, chunk_rows), :]
        cbuf[pl.ds(pl.multiple_of(t0 * n_grp, chunk_rows), chunk_rows), :] = acc
        return 0

    lax.fori_loop(0, tm // CONV_TIME_CHUNK, time_chunk, 0)

    hc = jnp.concatenate([cbuf[pl.ds(grp, tm, stride=n_grp), :] for grp in range(n_grp)], axis=1)
    mu = jnp.mean(hc, axis=-1, keepdims=True)
    d = hc - mu
    var = jnp.mean(d * d, axis=-1, keepdims=True)
    y = d * lax.rsqrt(var + LN_EPS) * lng_ref[...] + lnb_ref[...]
    y = y * jax.nn.sigmoid(y)
    p = jnp.dot(y.astype(BF16), w2_ref[...], preferred_element_type=F32) + b2_ref[...]
    for grp in range(n_grp):
        cols = slice(grp * V7X_LANES, (grp + 1) * V7X_LANES)
        zc = zc_ref[grp].astype(F32)
        o_ref[:, cols] = (p[:, cols] * (zc * jax.nn.sigmoid(zc))).astype(o_ref.dtype)


def _conv_glu(proj, conv_w, conv_b, ln_g, ln_b, w_pw2, b_pw2, *, u_blk, tm=512):
    s = proj.shape[1]
    c = w_pw2.shape[0]
    n_grp = c // V7X_LANES
    halo_blocks = tm // CONV_HALO
    est = (2 * 3 * tm * c * 2 + (tm + CONV_HALO) * c * 4 + tm * c * 4 + 2 * c * c * 2
           + 2 * tm * c * 2 + 4 * tm * c * 4)
    cur = lambda off: pl.BlockSpec((n_grp, tm, V7X_LANES), lambda i: (off, i, 0))
    prev = lambda off: pl.BlockSpec(
        (n_grp, CONV_HALO, V7X_LANES), lambda i: (off, jnp.maximum(i * halo_blocks - 1, 0), 0))
    row = lambda n: pl.BlockSpec((n, c), lambda i: (0, 0))
    tmaj = lambda n: pl.BlockSpec((n * n_grp, V7X_LANES), lambda i: (0, 0))
    return pl.pallas_call(
        functools.partial(_conv_kernel, tm=tm),
        out_shape=jax.ShapeDtypeStruct((s, c), BF16),
        grid=(s // tm,),
        in_specs=[cur(u_blk), cur(u_blk + 1), cur(u_blk + 2), prev(u_blk), prev(u_blk + 1),
                  tmaj(CONV_KERNEL), tmaj(1), row(1), row(1),
                  pl.BlockSpec((c, c), lambda i: (0, 0)), row(1)],
        out_specs=pl.BlockSpec((tm, c), lambda i: (i, 0)),
        scratch_shapes=[pltpu.VMEM(((tm + CONV_HALO) * n_grp, V7X_LANES), F32),
                        pltpu.VMEM((tm * n_grp, V7X_LANES), F32)],
        compiler_params=pltpu.CompilerParams(
            dimension_semantics=("arbitrary",),
            vmem_limit_bytes=_vmem_limit(est)),
        name="conv_glu",
    )(proj, proj, proj, proj, proj, conv_w, conv_b, ln_g, ln_b, w_pw2, b_pw2)


def _out_proj_kernel(ya_ref, yc_ref, w_ref, x_ref, g_ref, o_ref, wb_scr):
    @pl.when(pl.program_id(0) == 0)
    def _():
        wb_scr[...] = w_ref[...].astype(BF16)

    ca = ya_ref.shape[1]
    y = (jnp.dot(ya_ref[...], wb_scr[:ca, :], preferred_element_type=F32)
         + jnp.dot(yc_ref[...], wb_scr[ca:, :], preferred_element_type=F32))
    ms = jnp.mean(y * y, axis=-1, keepdims=True)
    o_ref[...] = x_ref[...] + y * lax.rsqrt(ms + RMS_EPS) * g_ref[...]


def _out_proj(y_att, y_conv, w_out, x, g_post, *, tm=512):
    s, d = x.shape
    ca, cc = y_att.shape[1], y_conv.shape[1]
    est = (ca + cc) * d * (4 + 2) + 2 * tm * (ca + cc) * 2 + 4 * tm * d * 4 + 2 * tm * d * 4
    return pl.pallas_call(
        _out_proj_kernel,
        out_shape=jax.ShapeDtypeStruct((s, d), x.dtype),
        grid=(s // tm,),
        in_specs=[
            pl.BlockSpec((tm, ca), lambda i: (i, 0)),
            pl.BlockSpec((tm, cc), lambda i: (i, 0)),
            pl.BlockSpec((ca + cc, d), lambda i: (0, 0), pipeline_mode=pl.Buffered(1)),
            pl.BlockSpec((tm, d), lambda i: (i, 0)),
            pl.BlockSpec((1, d), lambda i: (0, 0)),
        ],
        out_specs=pl.BlockSpec((tm, d), lambda i: (i, 0)),
        scratch_shapes=[pltpu.VMEM((ca + cc, d), BF16)],
        compiler_params=pltpu.CompilerParams(
            dimension_semantics=("arbitrary",),
            vmem_limit_bytes=_vmem_limit(est)),
        name="out_proj",
    )(y_att, y_conv, w_out, x, g_post)


def kernel(x, g_pre, w_in, conv_w, conv_b, ln_g, ln_b, w_pw2, b_pw2, w_out, g_post):
    b, s, d = x.shape
    assert b == 1, "rows of different batch entries must not share a causal history"
    c = conv_w.shape[-1]
    aw = w_out.shape[0] - c
    n_heads = aw // HEAD_DIM
    assert w_in.shape[1] == 4 * aw + 3 * c and aw == c

    col_scale = jnp.concatenate(
        [jnp.full((aw,), LOG2_E / math.sqrt(HEAD_DIM), F32), jnp.ones((w_in.shape[1] - aw,), F32)])

    x2 = x.reshape(s, d)
    proj = _in_proj(x2, g_pre.reshape(1, d), w_in, col_scale.reshape(1, -1))
    y_att = _sb_attn(proj, n_heads=n_heads)
    y_conv = _conv_glu(proj, conv_w.reshape(-1, V7X_LANES), conv_b.reshape(-1, V7X_LANES),
                       ln_g.reshape(1, c), ln_b.reshape(1, c), w_pw2.astype(BF16),
                       b_pw2.reshape(1, c), u_blk=(4 * aw) // c)
    out = _out_proj(y_att, y_conv, w_out, x2, g_post.reshape(1, d))
    return out.reshape(b, s, d)
```

```python
import functools
import math

import jax
import jax.numpy as jnp
from jax import lax
from jax.experimental import pallas as pl
from jax.experimental.pallas import tpu as pltpu

F32 = jnp.float32
BF16 = jnp.bfloat16

HEAD_DIM = 128
CONV_KERNEL = 31
RMS_EPS = 1e-6
LN_EPS = 1e-5

V7X_LANES = 128
V7X_VMEM_BYTES = 64 * 1024 * 1024

CONV_HALO = 32
CONV_TIME_CHUNK = 32


def _vmem_limit(nbytes):
    return int(min(nbytes * 3 // 2 + (4 << 20), V7X_VMEM_BYTES - (6 << 20)))


X_PARTS = 4


def _in_proj_kernel(*refs):
    x_refs = refs[:X_PARTS]
    g_ref, w_ref, cs_ref, o_ref, h_scr = refs[X_PARTS:]

    @pl.when(pl.program_id(1) == 0)
    def _():
        part = x_refs[0].shape[0]
        for n, x_ref in enumerate(x_refs):
            x = x_ref[...]
            ms = jnp.mean(x * x, axis=-1, keepdims=True)
            h_scr[n * part:(n + 1) * part, :] = (x * lax.rsqrt(ms + RMS_EPS) * g_ref[...]).astype(BF16)

    acc = (jnp.dot(h_scr[...], w_ref[...].astype(BF16), preferred_element_type=F32)
           * cs_ref[...]).astype(o_ref.dtype)
    for grp in range(o_ref.shape[0]):
        o_ref[grp] = acc[:, grp * V7X_LANES:(grp + 1) * V7X_LANES]


def _in_proj(x, g_pre, w_in, col_scale, *, tm=1024, tn=1024):
    s, d = x.shape
    d_in = w_in.shape[1]
    n_i, n_j = s // tm, d_in // tn
    grp_per_tile = tn // V7X_LANES
    est = 2 * tm * d * 4 + tm * d * 2 + 2 * d * tn * 4 + 2 * tm * tn * 2

    assert n_j >= X_PARTS

    def x_part(which):
        lead = X_PARTS - which
        return pl.BlockSpec(
            (tm // X_PARTS, d),
            lambda i, j: (X_PARTS * jnp.minimum(i + (j >= n_j - lead).astype(jnp.int32), n_i - 1)
                          + which, 0))

    return pl.pallas_call(
        _in_proj_kernel,
        out_shape=jax.ShapeDtypeStruct((d_in // V7X_LANES, s, V7X_LANES), BF16),
        grid=(n_i, n_j),
        in_specs=[x_part(n) for n in range(X_PARTS)] + [
            pl.BlockSpec((1, d), lambda i, j: (0, 0)),
            pl.BlockSpec((d, tn), lambda i, j: (0, j)),
            pl.BlockSpec((1, tn), lambda i, j: (0, j)),
        ],
        out_specs=pl.BlockSpec((grp_per_tile, tm, V7X_LANES), lambda i, j: (j, i, 0)),
        scratch_shapes=[pltpu.VMEM((tm, d), BF16)],
        compiler_params=pltpu.CompilerParams(
            dimension_semantics=("arbitrary", "arbitrary"),
            vmem_limit_bytes=_vmem_limit(est)),
        name="in_proj",
    )(*([x] * X_PARTS), g_pre, w_in, col_scale)


LOG2_E = math.log2(math.e)
F32_EXP2_UNDERFLOW = -151.5
NO_KEY_BLOCK = 1e30


def _softplus2(z2):
    return jnp.maximum(z2, 0.0) + jnp.log(1.0 + jnp.exp2(-jnp.abs(z2))) * LOG2_E


def _attn_kernel(q_ref, k_ref, v_ref, zg_ref, o_ref, acc_ref, tail_ref, *, tb, n_sub):
    i = pl.program_id(1)
    row = lax.broadcasted_iota(jnp.int32, (tb, tb), 0)
    col = lax.broadcasted_iota(jnp.int32, (tb, tb), 1)
    later = (row > col).astype(BF16)
    causal = col < row

    def key_block(q, j, tail, diagonal):
        ks = pl.multiple_of(j * tb, tb)
        k = k_ref[pl.ds(ks, tb), :]
        v = v_ref[pl.ds(ks, tb), :]
        z2 = lax.dot_general(q, k, (((1,), (1,)), ((), ())), preferred_element_type=F32)
        sp = _softplus2(z2)
        log_beta = z2 - sp
        if diagonal:
            sp = jnp.where(causal, sp, 0.0)
        sp_hi = sp.astype(BF16)
        sp_lo = (sp - sp_hi.astype(F32)).astype(BF16)
        inner = (jnp.dot(sp_hi, later, preferred_element_type=F32)
                 + jnp.dot(sp_lo, later, preferred_element_type=F32))
        if diagonal:
            a = jnp.where(causal, jnp.exp2(log_beta - inner), 0.0)
            tail = -jnp.sum(sp, axis=-1, keepdims=True)
        else:
            a = jnp.exp2(log_beta - inner + tail)
            tail = tail - jnp.sum(sp, axis=-1, keepdims=True)
        pv = jnp.dot(a.astype(BF16), v, preferred_element_type=F32)
        return pv, tail

    for sb in range(n_sub):
        rows = slice(sb * tb, (sb + 1) * tb)
        qb = i * n_sub + sb
        q = q_ref[rows, :]
        pv_d, tail = key_block(q, qb, None, diagonal=True)
        tail = tail - jnp.where(qb > 0, 0.0, NO_KEY_BLOCK)
        pv_p, tail = key_block(q, jnp.maximum(qb - 1, 0), tail, diagonal=False)
        acc_ref[rows, :] = pv_d + pv_p
        tail_ref[rows, :] = tail

    def more(state):
        step, live = state
        return jnp.logical_and((i + 1) * n_sub - 3 - step >= 0, live)

    def walk(state):
        step, _ = state
        for sb in range(n_sub):
            rows = slice(sb * tb, (sb + 1) * tb)
            j = i * n_sub + sb - 2 - step

            @pl.when(j >= 0)
            def _():
                pv, tail = key_block(q_ref[rows, :], j, tail_ref[rows, :], diagonal=False)
                acc_ref[rows, :] += pv
                tail_ref[rows, :] = tail
        return step + 1, jnp.max(tail_ref[...]) >= F32_EXP2_UNDERFLOW

    lax.while_loop(more, walk, (0, jnp.max(tail_ref[...]) >= F32_EXP2_UNDERFLOW))

    zg = zg_ref[...].astype(F32)
    o_ref[...] = (acc_ref[...] * (zg * jax.nn.sigmoid(zg))).astype(o_ref.dtype)


def _sb_attn(proj, *, n_heads, tb=256, n_sub=32):
    s = proj.shape[1]
    tq = tb * n_sub
    est = 2 * 2 * s * HEAD_DIM * 2 + 6 * tq * HEAD_DIM * 2 + n_sub * 16 * tb * tb * 4
    qspec = lambda off: pl.BlockSpec((None, tq, HEAD_DIM), lambda h, i: (off + h, i, 0))
    kvspec = lambda off: pl.BlockSpec((None, s, HEAD_DIM), lambda h, i: (off + h, 0, 0))
    return pl.pallas_call(
        functools.partial(_attn_kernel, tb=tb, n_sub=n_sub),
        out_shape=jax.ShapeDtypeStruct((s, n_heads * HEAD_DIM), BF16),
        grid=(n_heads, s // tq),
        in_specs=[qspec(0), kvspec(n_heads), kvspec(2 * n_heads), qspec(3 * n_heads)],
        out_specs=pl.BlockSpec((tq, HEAD_DIM), lambda h, i: (i, h)),
        scratch_shapes=[pltpu.VMEM((tq, HEAD_DIM), F32), pltpu.VMEM((tq, 1), F32)],
        compiler_params=pltpu.CompilerParams(
            dimension_semantics=("arbitrary", "arbitrary"),
            vmem_limit_bytes=_vmem_limit(est)),
        name="sb_attn",
    )(proj, proj, proj, proj)


def _conv_kernel(u_ref, g_ref, zc_ref, up_ref, gp_ref, cw_ref, cb_ref, lng_ref, lnb_ref,
                 w2_ref, b2_ref, o_ref, hbuf, cbuf, w2b_scr, *, tm):
    i = pl.program_id(0)

    @pl.when(i == 0)
    def _():
        w2b_scr[...] = w2_ref[...].astype(BF16)

    n_grp = u_ref.shape[0]
    for grp in range(n_grp):
        h_cur = u_ref[grp].astype(F32) * jax.nn.sigmoid(g_ref[grp].astype(F32))
        hbuf[pl.ds(CONV_HALO * n_grp + grp, tm, stride=n_grp), :] = h_cur
        h_prev = up_ref[grp].astype(F32) * jax.nn.sigmoid(gp_ref[grp].astype(F32))
        hbuf[pl.ds(grp, CONV_HALO, stride=n_grp), :] = jnp.where(i > 0, h_prev, 0.0)

    first = CONV_HALO - (CONV_KERNEL - 1)
    chunk_rows = CONV_TIME_CHUNK * n_grp

    def time_chunk(tc, _):
        t0 = tc * CONV_TIME_CHUNK
        acc = jnp.tile(cb_ref[...], (CONV_TIME_CHUNK, 1))
        for k in range(CONV_KERNEL):
            start = pl.multiple_of((t0 + first + k) * n_grp, n_grp)
            tap = jnp.tile(cw_ref[k * n_grp:(k + 1) * n_grp, :], (CONV_TIME_CHUNK, 1))
            acc = acc + tap * hbuf[pl.ds(start, chunk_rows), :]
        cbuf[pl.ds(pl.multiple_of(t0 * n_grp, chunk_rows), chunk_rows), :] = acc
        return 0

    lax.fori_loop(0, tm // CONV_TIME_CHUNK, time_chunk, 0)

    hc = jnp.concatenate([cbuf[pl.ds(grp, tm, stride=n_grp), :] for grp in range(n_grp)], axis=1)
    mu = jnp.mean(hc, axis=-1, keepdims=True)
    d = hc - mu
    var = jnp.mean(d * d, axis=-1, keepdims=True)
    y = d * lax.rsqrt(var + LN_EPS) * lng_ref[...] + lnb_ref[...]
    y = y * jax.nn.sigmoid(y)
    p = jnp.dot(y.astype(BF16), w2b_scr[...], preferred_element_type=F32) + b2_ref[...]
    for grp in range(n_grp):
        cols = slice(grp * V7X_LANES, (grp + 1) * V7X_LANES)
        zc = zc_ref[grp].astype(F32)
        o_ref[:, cols] = (p[:, cols] * (zc * jax.nn.sigmoid(zc))).astype(o_ref.dtype)


def _conv_glu(proj, conv_w, conv_b, ln_g, ln_b, w_pw2, b_pw2, *, u_blk, tm=512):
    s = proj.shape[1]
    c = w_pw2.shape[0]
    n_grp = c // V7X_LANES
    halo_blocks = tm // CONV_HALO
    est = (2 * 3 * tm * c * 2 + (tm + CONV_HALO) * c * 4 + tm * c * 4 + 2 * c * c * 2
           + 2 * tm * c * 2 + 4 * tm * c * 4)
    cur = lambda off: pl.BlockSpec((n_grp, tm, V7X_LANES), lambda i: (off, i, 0))
    prev = lambda off: pl.BlockSpec(
        (n_grp, CONV_HALO, V7X_LANES), lambda i: (off, jnp.maximum(i * halo_blocks - 1, 0), 0))
    row = lambda n: pl.BlockSpec((n, c), lambda i: (0, 0))
    tmaj = lambda n: pl.BlockSpec((n * n_grp, V7X_LANES), lambda i: (0, 0))
    return pl.pallas_call(
        functools.partial(_conv_kernel, tm=tm),
        out_shape=jax.ShapeDtypeStruct((s, c), BF16),
        grid=(s // tm,),
        in_specs=[cur(u_blk), cur(u_blk + 1), cur(u_blk + 2), prev(u_blk), prev(u_blk + 1),
                  tmaj(CONV_KERNEL), tmaj(1), row(1), row(1),
                  pl.BlockSpec((c, c), lambda i: (0, 0), pipeline_mode=pl.Buffered(1)), row(1)],
        out_specs=pl.BlockSpec((tm, c), lambda i: (i, 0)),
        scratch_shapes=[pltpu.VMEM(((tm + CONV_HALO) * n_grp, V7X_LANES), F32),
                        pltpu.VMEM((tm * n_grp, V7X_LANES), F32),
                        pltpu.VMEM((c, c), BF16)],
        compiler_params=pltpu.CompilerParams(
            dimension_semantics=("arbitrary",),
            vmem_limit_bytes=_vmem_limit(est)),
        name="conv_glu",
    )(proj, proj, proj, proj, proj, conv_w, conv_b, ln_g, ln_b, w_pw2, b_pw2)


def _out_proj_kernel(ya_ref, yc_ref, w_ref, x_ref, g_ref, o_ref, wb_scr):
    @pl.when(pl.program_id(0) == 0)
    def _():
        wb_scr[...] = w_ref[...].astype(BF16)

    ca = ya_ref.shape[1]
    y = (jnp.dot(ya_ref[...], wb_scr[:ca, :], preferred_element_type=F32)
         + jnp.dot(yc_ref[...], wb_scr[ca:, :], preferred_element_type=F32))
    ms = jnp.mean(y * y, axis=-1, keepdims=True)
    o_ref[...] = x_ref[...] + y * lax.rsqrt(ms + RMS_EPS) * g_ref[...]


def _out_proj(y_att, y_conv, w_out, x, g_post, *, tm=512):
    s, d = x.shape
    ca, cc = y_att.shape[1], y_conv.shape[1]
    est = (ca + cc) * d * (4 + 2) + 2 * tm * (ca + cc) * 2 + 4 * tm * d * 4 + 2 * tm * d * 4
    return pl.pallas_call(
        _out_proj_kernel,
        out_shape=jax.ShapeDtypeStruct((s, d), x.dtype),
        grid=(s // tm,),
        in_specs=[
            pl.BlockSpec((tm, ca), lambda i: (i, 0)),
            pl.BlockSpec((tm, cc), lambda i: (i, 0)),
            pl.BlockSpec((ca + cc, d), lambda i: (0, 0), pipeline_mode=pl.Buffered(1)),
            pl.BlockSpec((tm, d), lambda i: (i, 0)),
            pl.BlockSpec((1, d), lambda i: (0, 0)),
        ],
        out_specs=pl.BlockSpec((tm, d), lambda i: (i, 0)),
        scratch_shapes=[pltpu.VMEM((ca + cc, d), BF16)],
        compiler_params=pltpu.CompilerParams(
            dimension_semantics=("arbitrary",),
            vmem_limit_bytes=_vmem_limit(est)),
        name="out_proj",
    )(y_att, y_conv, w_out, x, g_post)


def kernel(x, g_pre, w_in, conv_w, conv_b, ln_g, ln_b, w_pw2, b_pw2, w_out, g_post):
    b, s, d = x.shape
    assert b == 1, "rows of different batch entries must not share a causal history"
    c = conv_w.shape[-1]
    aw = w_out.shape[0] - c
    n_heads = aw // HEAD_DIM
    assert w_in.shape[1] == 4 * aw + 3 * c and aw == c

    col_scale = jnp.concatenate(
        [jnp.full((aw,), LOG2_E / math.sqrt(HEAD_DIM), F32), jnp.ones((w_in.shape[1] - aw,), F32)])

    x2 = x.reshape(s, d)
    proj = _in_proj(x2, g_pre.reshape(1, d), w_in, col_scale.reshape(1, -1))
    y_att = _sb_attn(proj, n_heads=n_heads)
    y_conv = _conv_glu(proj, conv_w.reshape(-1, V7X_LANES), conv_b.reshape(-1, V7X_LANES),
                       ln_g.reshape(1, c), ln_b.reshape(1, c), w_pw2,
                       b_pw2.reshape(1, c), u_blk=(4 * aw) // c)
    out = _out_proj(y_att, y_conv, w_out, x2, g_post.reshape(1, d))
    return out.reshape(b, s, d)
```

```python
import functools
import math

import jax
import jax.numpy as jnp
from jax import lax
from jax.experimental import pallas as pl
from jax.experimental.pallas import tpu as pltpu

F32 = jnp.float32
BF16 = jnp.bfloat16

HEAD_DIM = 128
CONV_KERNEL = 31
RMS_EPS = 1e-6
LN_EPS = 1e-5

V7X_LANES = 128
V7X_VMEM_BYTES = 64 * 1024 * 1024

CONV_HALO = 32
CONV_TIME_CHUNK = 32


def _vmem_limit(nbytes):
    return int(min(nbytes * 3 // 2 + (4 << 20), V7X_VMEM_BYTES - (6 << 20)))


X_PARTS = 4


def _in_proj_kernel(*refs):
    x_refs = refs[:X_PARTS]
    g_ref, w_ref, cs_ref, o_ref, h_scr = refs[X_PARTS:]

    @pl.when(pl.program_id(1) == 0)
    def _():
        part = x_refs[0].shape[0]
        for n, x_ref in enumerate(x_refs):
            x = x_ref[...]
            ms = jnp.mean(x * x, axis=-1, keepdims=True)
            h_scr[n * part:(n + 1) * part, :] = (x * lax.rsqrt(ms + RMS_EPS) * g_ref[...]).astype(BF16)

    acc = (jnp.dot(h_scr[...], w_ref[...].astype(BF16), preferred_element_type=F32)
           * cs_ref[...]).astype(o_ref.dtype)
    for grp in range(o_ref.shape[0]):
        o_ref[grp] = acc[:, grp * V7X_LANES:(grp + 1) * V7X_LANES]


def _in_proj(x, g_pre, w_in, col_scale, *, tm=1024, tn=1024):
    s, d = x.shape
    d_in = w_in.shape[1]
    n_i, n_j = s // tm, d_in // tn
    grp_per_tile = tn // V7X_LANES
    est = 2 * tm * d * 4 + tm * d * 2 + 2 * d * tn * 4 + 2 * tm * tn * 2

    assert n_j >= X_PARTS

    def x_part(which):
        lead = X_PARTS - which
        return pl.BlockSpec(
            (tm // X_PARTS, d),
            lambda i, j: (X_PARTS * jnp.minimum(i + (j >= n_j - lead).astype(jnp.int32), n_i - 1)
                          + which, 0))

    return pl.pallas_call(
        _in_proj_kernel,
        out_shape=jax.ShapeDtypeStruct((d_in // V7X_LANES, s, V7X_LANES), BF16),
        grid=(n_i, n_j),
        in_specs=[x_part(n) for n in range(X_PARTS)] + [
            pl.BlockSpec((1, d), lambda i, j: (0, 0)),
            pl.BlockSpec((d, tn), lambda i, j: (0, j)),
            pl.BlockSpec((1, tn), lambda i, j: (0, j)),
        ],
        out_specs=pl.BlockSpec((grp_per_tile, tm, V7X_LANES), lambda i, j: (j, i, 0)),
        scratch_shapes=[pltpu.VMEM((tm, d), BF16)],
        compiler_params=pltpu.CompilerParams(
            dimension_semantics=("arbitrary", "arbitrary"),
            vmem_limit_bytes=_vmem_limit(est)),
        name="in_proj",
    )(*([x] * X_PARTS), g_pre, w_in, col_scale)


LOG2_E = math.log2(math.e)
F32_EXP2_UNDERFLOW = -151.5
NO_KEY_BLOCK = 1e30


def _softplus2(z2):
    return jnp.maximum(z2, 0.0) + jnp.log(1.0 + jnp.exp2(-jnp.abs(z2))) * LOG2_E


def _attn_kernel(q_ref, k_ref, v_ref, zg_ref, o_ref, acc_ref, tail_ref, *, tb, n_sub):
    i = pl.program_id(1)
    row = lax.broadcasted_iota(jnp.int32, (tb, tb), 0)
    col = lax.broadcasted_iota(jnp.int32, (tb, tb), 1)
    later = (row > col).astype(BF16)
    causal = col < row

    def key_block(q, j, tail, diagonal):
        ks = pl.multiple_of(j * tb, tb)
        k = k_ref[pl.ds(ks, tb), :]
        v = v_ref[pl.ds(ks, tb), :]
        z2 = lax.dot_general(q, k, (((1,), (1,)), ((), ())), preferred_element_type=F32)
        sp = _softplus2(z2)
        log_beta = z2 - sp
        if diagonal:
            sp = jnp.where(causal, sp, 0.0)
        sp_hi = sp.astype(BF16)
        sp_lo = (sp - sp_hi.astype(F32)).astype(BF16)
        both = jnp.dot(jnp.concatenate([sp_hi, sp_lo], axis=0), later, preferred_element_type=F32)
        inner = both[:tb] + both[tb:]
        if diagonal:
            a = jnp.where(causal, jnp.exp2(log_beta - inner), 0.0)
            tail = -jnp.sum(sp, axis=-1, keepdims=True)
        else:
            a = jnp.exp2(log_beta - inner + tail)
            tail = tail - jnp.sum(sp, axis=-1, keepdims=True)
        pv = jnp.dot(a.astype(BF16), v, preferred_element_type=F32)
        return pv, tail

    for sb in range(n_sub):
        rows = slice(sb * tb, (sb + 1) * tb)
        qb = i * n_sub + sb
        q = q_ref[rows, :]
        pv_d, tail = key_block(q, qb, None, diagonal=True)
        tail = tail - jnp.where(qb > 0, 0.0, NO_KEY_BLOCK)
        pv_p, tail = key_block(q, jnp.maximum(qb - 1, 0), tail, diagonal=False)
        acc_ref[rows, :] = pv_d + pv_p
        tail_ref[rows, :] = tail

    def more(state):
        step, live = state
        return jnp.logical_and((i + 1) * n_sub - 3 - step >= 0, live)

    def walk(state):
        step, _ = state
        for sb in range(n_sub):
            rows = slice(sb * tb, (sb + 1) * tb)
            j = i * n_sub + sb - 2 - step

            @pl.when(j >= 0)
            def _():
                pv, tail = key_block(q_ref[rows, :], j, tail_ref[rows, :], diagonal=False)
                acc_ref[rows, :] += pv
                tail_ref[rows, :] = tail
        return step + 1, jnp.max(tail_ref[...]) >= F32_EXP2_UNDERFLOW

    lax.while_loop(more, walk, (0, jnp.max(tail_ref[...]) >= F32_EXP2_UNDERFLOW))

    zg = zg_ref[...].astype(F32)
    o_ref[...] = (acc_ref[...] * (zg * jax.nn.sigmoid(zg))).astype(o_ref.dtype)


def _sb_attn(proj, *, n_heads, tb=256, n_sub=16):
    s = proj.shape[1]
    tq = tb * n_sub
    est = 2 * 2 * s * HEAD_DIM * 2 + 6 * tq * HEAD_DIM * 2 + n_sub * 16 * tb * tb * 4
    qspec = lambda off: pl.BlockSpec((None, tq, HEAD_DIM), lambda h, i: (off + h, i, 0))
    kvspec = lambda off: pl.BlockSpec((None, s, HEAD_DIM), lambda h, i: (off + h, 0, 0))
    return pl.pallas_call(
        functools.partial(_attn_kernel, tb=tb, n_sub=n_sub),
        out_shape=jax.ShapeDtypeStruct((s, n_heads * HEAD_DIM), BF16),
        grid=(n_heads, s // tq),
        in_specs=[qspec(0), kvspec(n_heads), kvspec(2 * n_heads), qspec(3 * n_heads)],
        out_specs=pl.BlockSpec((tq, HEAD_DIM), lambda h, i: (i, h)),
        scratch_shapes=[pltpu.VMEM((tq, HEAD_DIM), F32), pltpu.VMEM((tq, 1), F32)],
        compiler_params=pltpu.CompilerParams(
            dimension_semantics=("arbitrary", "arbitrary"),
            vmem_limit_bytes=_vmem_limit(est)),
        name="sb_attn",
    )(proj, proj, proj, proj)


def _conv_kernel(u_ref, g_ref, zc_ref, up_ref, gp_ref, cw_ref, cb_ref, lng_ref, lnb_ref,
                 w2_ref, b2_ref, o_ref, hbuf, cbuf, w2b_scr, *, tm):
    i = pl.program_id(0)

    @pl.when(i == 0)
    def _():
        w2b_scr[...] = w2_ref[...].astype(BF16)

    n_grp = u_ref.shape[0]
    for grp in range(n_grp):
        h_cur = u_ref[grp].astype(F32) * jax.nn.sigmoid(g_ref[grp].astype(F32))
        hbuf[pl.ds(CONV_HALO * n_grp + grp, tm, stride=n_grp), :] = h_cur
        h_prev = up_ref[grp].astype(F32) * jax.nn.sigmoid(gp_ref[grp].astype(F32))
        hbuf[pl.ds(grp, CONV_HALO, stride=n_grp), :] = jnp.where(i > 0, h_prev, 0.0)

    first = CONV_HALO - (CONV_KERNEL - 1)
    chunk_rows = CONV_TIME_CHUNK * n_grp

    def time_chunk(tc, _):
        t0 = tc * CONV_TIME_CHUNK
        acc = jnp.tile(cb_ref[...], (CONV_TIME_CHUNK, 1))
        for k in range(CONV_KERNEL):
            start = pl.multiple_of((t0 + first + k) * n_grp, n_grp)
            tap = jnp.tile(cw_ref[k * n_grp:(k + 1) * n_grp, :], (CONV_TIME_CHUNK, 1))
            acc = acc + tap * hbuf[pl.ds(start, chunk_rows), :]
        cbuf[pl.ds(pl.multiple_of(t0 * n_grp, chunk_rows), chunk_rows), :] = acc
        return 0

    lax.fori_loop(0, tm // CONV_TIME_CHUNK, time_chunk, 0)

    hc = jnp.concatenate([cbuf[pl.ds(grp, tm, stride=n_grp), :] for grp in range(n_grp)], axis=1)
    mu = jnp.mean(hc, axis=-1, keepdims=True)
    d = hc - mu
    var = jnp.mean(d * d, axis=-1, keepdims=True)
    y = d * lax.rsqrt(var + LN_EPS) * lng_ref[...] + lnb_ref[...]
    y = y * jax.nn.sigmoid(y)
    p = jnp.dot(y.astype(BF16), w2b_scr[...], preferred_element_type=F32) + b2_ref[...]
    for grp in range(n_grp):
        cols = slice(grp * V7X_LANES, (grp + 1) * V7X_LANES)
        zc = zc_ref[grp].astype(F32)
        o_ref[:, cols] = (p[:, cols] * (zc * jax.nn.sigmoid(zc))).astype(o_ref.dtype)


def _conv_glu(proj, conv_w, conv_b, ln_g, ln_b, w_pw2, b_pw2, *, u_blk, tm=512):
    s = proj.shape[1]
    c = w_pw2.shape[0]
    n_grp = c // V7X_LANES
    halo_blocks = tm // CONV_HALO
    est = (2 * 3 * tm * c * 2 + (tm + CONV_HALO) * c * 4 + tm * c * 4 + 2 * c * c * 2
           + 2 * tm * c * 2 + 4 * tm * c * 4)
    cur = lambda off: pl.BlockSpec((n_grp, tm, V7X_LANES), lambda i: (off, i, 0))
    prev = lambda off: pl.BlockSpec(
        (n_grp, CONV_HALO, V7X_LANES), lambda i: (off, jnp.maximum(i * halo_blocks - 1, 0), 0))
    row = lambda n: pl.BlockSpec((n, c), lambda i: (0, 0))
    tmaj = lambda n: pl.BlockSpec((n * n_grp, V7X_LANES), lambda i: (0, 0))
    return pl.pallas_call(
        functools.partial(_conv_kernel, tm=tm),
        out_shape=jax.ShapeDtypeStruct((s, c), BF16),
        grid=(s // tm,),
        in_specs=[cur(u_blk), cur(u_blk + 1), cur(u_blk + 2), prev(u_blk), prev(u_blk + 1),
                  tmaj(CONV_KERNEL), tmaj(1), row(1), row(1),
                  pl.BlockSpec((c, c), lambda i: (0, 0), pipeline_mode=pl.Buffered(1)), row(1)],
        out_specs=pl.BlockSpec((tm, c), lambda i: (i, 0)),
        scratch_shapes=[pltpu.VMEM(((tm + CONV_HALO) * n_grp, V7X_LANES), F32),
                        pltpu.VMEM((tm * n_grp, V7X_LANES), F32),
                        pltpu.VMEM((c, c), BF16)],
        compiler_params=pltpu.CompilerParams(
            dimension_semantics=("arbitrary",),
            vmem_limit_bytes=_vmem_limit(est)),
        name="conv_glu",
    )(proj, proj, proj, proj, proj, conv_w, conv_b, ln_g, ln_b, w_pw2, b_pw2)


def _out_proj_kernel(ya_ref, yc_ref, w_ref, x_ref, g_ref, o_ref, wb_scr):
    @pl.when(pl.program_id(0) == 0)
    def _():
        wb_scr[...] = w_ref[...].astype(BF16)

    ca = ya_ref.shape[1]
    y = (jnp.dot(ya_ref[...], wb_scr[:ca, :], preferred_element_type=F32)
         + jnp.dot(yc_ref[...], wb_scr[ca:, :], preferred_element_type=F32))
    ms = jnp.mean(y * y, axis=-1, keepdims=True)
    o_ref[...] = x_ref[...] + y * lax.rsqrt(ms + RMS_EPS) * g_ref[...]


def _out_proj(y_att, y_conv, w_out, x, g_post, *, tm=512):
    s, d = x.shape
    ca, cc = y_att.shape[1], y_conv.shape[1]
    est = (ca + cc) * d * (4 + 2) + 2 * tm * (ca + cc) * 2 + 4 * tm * d * 4 + 2 * tm * d * 4
    return pl.pallas_call(
        _out_proj_kernel,
        out_shape=jax.ShapeDtypeStruct((s, d), x.dtype),
        grid=(s // tm,),
        in_specs=[
            pl.BlockSpec((tm, ca), lambda i: (i, 0)),
            pl.BlockSpec((tm, cc), lambda i: (i, 0)),
            pl.BlockSpec((ca + cc, d), lambda i: (0, 0), pipeline_mode=pl.Buffered(1)),
            pl.BlockSpec((tm, d), lambda i: (i, 0)),
            pl.BlockSpec((1, d), lambda i: (0, 0)),
        ],
        out_specs=pl.BlockSpec((tm, d), lambda i: (i, 0)),
        scratch_shapes=[pltpu.VMEM((ca + cc, d), BF16)],
        compiler_params=pltpu.CompilerParams(
            dimension_semantics=("arbitrary",),
            vmem_limit_bytes=_vmem_limit(est)),
        name="out_proj",
    )(y_att, y_conv, w_out, x, g_post)


def kernel(x, g_pre, w_in, conv_w, conv_b, ln_g, ln_b, w_pw2, b_pw2, w_out, g_post):
    b, s, d = x.shape
    assert b == 1, "rows of different batch entries must not share a causal history"
    c = conv_w.shape[-1]
    aw = w_out.shape[0] - c
    n_heads = aw // HEAD_DIM
    assert w_in.shape[1] == 4 * aw + 3 * c and aw == c

    col_scale = jnp.concatenate(
        [jnp.full((aw,), LOG2_E / math.sqrt(HEAD_DIM), F32), jnp.ones((w_in.shape[1] - aw,), F32)])

    x2 = x.reshape(s, d)
    proj = _in_proj(x2, g_pre.reshape(1, d), w_in, col_scale.reshape(1, -1))
    y_att = _sb_attn(proj, n_heads=n_heads)
    y_conv = _conv_glu(proj, conv_w.reshape(-1, V7X_LANES), conv_b.reshape(-1, V7X_LANES),
                       ln_g.reshape(1, c), ln_b.reshape(1, c), w_pw2,
                       b_pw2.reshape(1, c), u_blk=(4 * aw) // c)
    out = _out_proj(y_att, y_conv, w_out, x2, g_post.reshape(1, d))
    return out.reshape(b, s, d)
```

```python
import functools
import math

import jax
import jax.numpy as jnp
from jax import lax
from jax.experimental import pallas as pl
from jax.experimental.pallas import tpu as pltpu

F32 = jnp.float32
BF16 = jnp.bfloat16

HEAD_DIM = 128
CONV_KERNEL = 31
RMS_EPS = 1e-6
LN_EPS = 1e-5

V7X_LANES = 128
V7X_VMEM_BYTES = 64 * 1024 * 1024

CONV_HALO = 32
CONV_TIME_CHUNK = 32


def _vmem_limit(nbytes):
    return int(min(nbytes * 3 // 2 + (4 << 20), V7X_VMEM_BYTES - (6 << 20)))


X_PARTS = 4


def _in_proj_kernel(*refs):
    x_refs = refs[:X_PARTS]
    g_ref, w_ref, cs_ref, o_ref, h_scr = refs[X_PARTS:]

    @pl.when(pl.program_id(1) == 0)
    def _():
        part = x_refs[0].shape[0]
        for n, x_ref in enumerate(x_refs):
            x = x_ref[...]
            ms = jnp.mean(x * x, axis=-1, keepdims=True)
            h_scr[n * part:(n + 1) * part, :] = (x * lax.rsqrt(ms + RMS_EPS) * g_ref[...]).astype(BF16)

    acc = (jnp.dot(h_scr[...], w_ref[...].astype(BF16), preferred_element_type=F32)
           * cs_ref[...]).astype(o_ref.dtype)
    for grp in range(o_ref.shape[0]):
        o_ref[grp] = acc[:, grp * V7X_LANES:(grp + 1) * V7X_LANES]


def _in_proj(x, g_pre, w_in, col_scale, *, tm=1024, tn=1024):
    s, d = x.shape
    d_in = w_in.shape[1]
    n_i, n_j = s // tm, d_in // tn
    grp_per_tile = tn // V7X_LANES
    est = 2 * tm * d * 4 + tm * d * 2 + 2 * d * tn * 4 + 2 * tm * tn * 2

    assert n_j >= X_PARTS

    def x_part(which):
        lead = X_PARTS - which
        return pl.BlockSpec(
            (tm // X_PARTS, d),
            lambda i, j: (X_PARTS * jnp.minimum(i + (j >= n_j - lead).astype(jnp.int32), n_i - 1)
                          + which, 0))

    return pl.pallas_call(
        _in_proj_kernel,
        out_shape=jax.ShapeDtypeStruct((d_in // V7X_LANES, s, V7X_LANES), BF16),
        grid=(n_i, n_j),
        in_specs=[x_part(n) for n in range(X_PARTS)] + [
            pl.BlockSpec((1, d), lambda i, j: (0, 0)),
            pl.BlockSpec((d, tn), lambda i, j: (0, j)),
            pl.BlockSpec((1, tn), lambda i, j: (0, j)),
        ],
        out_specs=pl.BlockSpec((grp_per_tile, tm, V7X_LANES), lambda i, j: (j, i, 0)),
        scratch_shapes=[pltpu.VMEM((tm, d), BF16)],
        compiler_params=pltpu.CompilerParams(
            dimension_semantics=("arbitrary", "arbitrary"),
            vmem_limit_bytes=_vmem_limit(est)),
        name="in_proj",
    )(*([x] * X_PARTS), g_pre, w_in, col_scale)


LOG2_E = math.log2(math.e)
F32_EXP2_UNDERFLOW = -151.5
NO_KEY_BLOCK = 1e30


def _softplus2(z2):
    return jnp.maximum(z2, 0.0) + jnp.log(1.0 + jnp.exp2(-jnp.abs(z2))) * LOG2_E


def _attn_kernel(q_ref, k_ref, v_ref, zg_ref, o_ref, acc_ref, tail_ref, *, tb, n_sub):
    i = pl.program_id(1)
    row = lax.broadcasted_iota(jnp.int32, (tb, tb), 0)
    col = lax.broadcasted_iota(jnp.int32, (tb, tb), 1)
    later = (row > col).astype(BF16)
    causal = col < row

    def key_block(q, j, tail, diagonal):
        ks = pl.multiple_of(j * tb, tb)
        k = k_ref[pl.ds(ks, tb), :]
        v = v_ref[pl.ds(ks, tb), :]
        z2 = lax.dot_general(q, k, (((1,), (1,)), ((), ())), preferred_element_type=F32)
        sp = _softplus2(z2)
        log_beta = z2 - sp
        if diagonal:
            sp = jnp.where(causal, sp, 0.0)
        sp_hi = sp.astype(BF16)
        sp_lo = (sp - sp_hi.astype(F32)).astype(BF16)
        both = jnp.dot(jnp.concatenate([sp_hi, sp_lo], axis=0), later, preferred_element_type=F32)
        inner = both[:tb] + both[tb:]
        if diagonal:
            a = jnp.where(causal, jnp.exp2(log_beta - inner), 0.0)
            tail = -jnp.sum(sp, axis=-1, keepdims=True)
        else:
            a = jnp.exp2(log_beta - inner + tail)
            tail = tail - jnp.sum(sp, axis=-1, keepdims=True)
        pv = jnp.dot(a.astype(BF16), v, preferred_element_type=F32)
        return pv, tail

    for sb in range(n_sub):
        rows = slice(sb * tb, (sb + 1) * tb)
        qb = i * n_sub + sb
        q = q_ref[rows, :]
        pv_d, tail = key_block(q, qb, None, diagonal=True)
        tail = tail - jnp.where(qb > 0, 0.0, NO_KEY_BLOCK)
        pv_p, tail = key_block(q, jnp.maximum(qb - 1, 0), tail, diagonal=False)
        acc_ref[rows, :] = pv_d + pv_p
        tail_ref[rows, :] = tail

    def more(state):
        step, live = state
        return jnp.logical_and((i + 1) * n_sub - 3 - step >= 0, live)

    def walk(state):
        step, _ = state
        for sb in range(n_sub):
            rows = slice(sb * tb, (sb + 1) * tb)
            j = i * n_sub + sb - 2 - step

            @pl.when(j >= 0)
            def _():
                pv, tail = key_block(q_ref[rows, :], j, tail_ref[rows, :], diagonal=False)
                acc_ref[rows, :] += pv
                tail_ref[rows, :] = tail
        return step + 1, jnp.max(tail_ref[...]) >= F32_EXP2_UNDERFLOW

    lax.while_loop(more, walk, (0, jnp.max(tail_ref[...]) >= F32_EXP2_UNDERFLOW))

    zg = zg_ref[...].astype(F32)
    o_ref[...] = (acc_ref[...] * (zg * jax.nn.sigmoid(zg))).astype(o_ref.dtype)


def _sb_attn(proj, *, n_heads, tb=256, n_sub=16):
    s = proj.shape[1]
    tq = tb * n_sub
    est = 2 * 2 * s * HEAD_DIM * 2 + 6 * tq * HEAD_DIM * 2 + n_sub * 16 * tb * tb * 4
    qspec = lambda off: pl.BlockSpec((None, tq, HEAD_DIM), lambda h, i: (off + h, i, 0))
    kvspec = lambda off: pl.BlockSpec((None, s, HEAD_DIM), lambda h, i: (off + h, 0, 0))
    return pl.pallas_call(
        functools.partial(_attn_kernel, tb=tb, n_sub=n_sub),
        out_shape=jax.ShapeDtypeStruct((s, n_heads * HEAD_DIM), BF16),
        grid=(n_heads, s // tq),
        in_specs=[qspec(0), kvspec(n_heads), kvspec(2 * n_heads), qspec(3 * n_heads)],
        out_specs=pl.BlockSpec((tq, HEAD_DIM), lambda h, i: (i, h)),
        scratch_shapes=[pltpu.VMEM((tq, HEAD_DIM), F32), pltpu.VMEM((tq, 1), F32)],
        compiler_params=pltpu.CompilerParams(
            dimension_semantics=("arbitrary", "arbitrary"),
            vmem_limit_bytes=_vmem_limit(est)),
        name="sb_attn",
    )(proj, proj, proj, proj)


def _conv_kernel(u_ref, g_ref, zc_ref, up_ref, gp_ref, cw_ref, cb_ref, lng_ref, lnb_ref,
                 w2_ref, b2_ref, ya_ref, wo_ref, x_ref, gpost_ref, o_ref,
                 hbuf, cbuf, w2b_scr, wob_scr, ycv, *, tm):
    i = pl.program_id(0)

    @pl.when(i == 0)
    def _():
        w2b_scr[...] = w2_ref[...].astype(BF16)
        wob_scr[...] = wo_ref[...].astype(BF16)

    n_grp = u_ref.shape[0]
    for grp in range(n_grp):
        h_cur = u_ref[grp].astype(F32) * jax.nn.sigmoid(g_ref[grp].astype(F32))
        hbuf[pl.ds(CONV_HALO * n_grp + grp, tm, stride=n_grp), :] = h_cur
        h_prev = up_ref[grp].astype(F32) * jax.nn.sigmoid(gp_ref[grp].astype(F32))
        hbuf[pl.ds(grp, CONV_HALO, stride=n_grp), :] = jnp.where(i > 0, h_prev, 0.0)

    first = CONV_HALO - (CONV_KERNEL - 1)
    chunk_rows = CONV_TIME_CHUNK * n_grp

    def time_chunk(tc, _):
        t0 = tc * CONV_TIME_CHUNK
        acc = jnp.tile(cb_ref[...], (CONV_TIME_CHUNK, 1))
        for k in range(CONV_KERNEL):
            start = pl.multiple_of((t0 + first + k) * n_grp, n_grp)
            tap = jnp.tile(cw_ref[k * n_grp:(k + 1) * n_grp, :], (CONV_TIME_CHUNK, 1))
            acc = acc + tap * hbuf[pl.ds(start, chunk_rows), :]
        cbuf[pl.ds(pl.multiple_of(t0 * n_grp, chunk_rows), chunk_rows), :] = acc
        return 0

    lax.fori_loop(0, tm // CONV_TIME_CHUNK, time_chunk, 0)

    hc = jnp.concatenate([cbuf[pl.ds(grp, tm, stride=n_grp), :] for grp in range(n_grp)], axis=1)
    mu = jnp.mean(hc, axis=-1, keepdims=True)
    d = hc - mu
    var = jnp.mean(d * d, axis=-1, keepdims=True)
    y = d * lax.rsqrt(var + LN_EPS) * lng_ref[...] + lnb_ref[...]
    y = y * jax.nn.sigmoid(y)
    p = jnp.dot(y.astype(BF16), w2b_scr[...], preferred_element_type=F32) + b2_ref[...]
    for grp in range(n_grp):
        cols = slice(grp * V7X_LANES, (grp + 1) * V7X_LANES)
        zc = zc_ref[grp].astype(F32)
        ycv[:, cols] = (p[:, cols] * (zc * jax.nn.sigmoid(zc))).astype(ycv.dtype)

    ca = ya_ref.shape[1]
    yo = (jnp.dot(ya_ref[...], wob_scr[:ca, :], preferred_element_type=F32)
          + jnp.dot(ycv[...], wob_scr[ca:, :], preferred_element_type=F32))
    ms = jnp.mean(yo * yo, axis=-1, keepdims=True)
    o_ref[...] = x_ref[...] + yo * lax.rsqrt(ms + RMS_EPS) * gpost_ref[...]


def _conv_out(proj, y_att, conv_w, conv_b, ln_g, ln_b, w_pw2, b_pw2, w_out, x, g_post, *, u_blk, tm=256):
    s = proj.shape[1]
    c = w_pw2.shape[0]
    n_grp = c // V7X_LANES
    halo_blocks = tm // CONV_HALO
    d = x.shape[1]
    ca = y_att.shape[1]
    est = (2 * 3 * tm * c * 2 + (tm + CONV_HALO) * c * 4 + tm * c * 4 + c * c * 6
           + (ca + c) * d * 6 + 2 * tm * ca * 2 + tm * c * 2 + 4 * tm * d * 4)
    cur = lambda off: pl.BlockSpec((n_grp, tm, V7X_LANES), lambda i: (off, i, 0))
    prev = lambda off: pl.BlockSpec(
        (n_grp, CONV_HALO, V7X_LANES), lambda i: (off, jnp.maximum(i * halo_blocks - 1, 0), 0))
    row = lambda n: pl.BlockSpec((n, c), lambda i: (0, 0))
    tmaj = lambda n: pl.BlockSpec((n * n_grp, V7X_LANES), lambda i: (0, 0))
    return pl.pallas_call(
        functools.partial(_conv_kernel, tm=tm),
        out_shape=jax.ShapeDtypeStruct((s, d), x.dtype),
        grid=(s // tm,),
        in_specs=[cur(u_blk), cur(u_blk + 1), cur(u_blk + 2), prev(u_blk), prev(u_blk + 1),
                  tmaj(CONV_KERNEL), tmaj(1), row(1), row(1),
                  pl.BlockSpec((c, c), lambda i: (0, 0), pipeline_mode=pl.Buffered(1)), row(1),
                  pl.BlockSpec((tm, ca), lambda i: (i, 0)),
                  pl.BlockSpec((ca + c, d), lambda i: (0, 0), pipeline_mode=pl.Buffered(1)),
                  pl.BlockSpec((tm, d), lambda i: (i, 0)),
                  pl.BlockSpec((1, d), lambda i: (0, 0))],
        out_specs=pl.BlockSpec((tm, d), lambda i: (i, 0)),
        scratch_shapes=[pltpu.VMEM(((tm + CONV_HALO) * n_grp, V7X_LANES), F32),
                        pltpu.VMEM((tm * n_grp, V7X_LANES), F32),
                        pltpu.VMEM((c, c), BF16),
                        pltpu.VMEM((ca + c, d), BF16),
                        pltpu.VMEM((tm, c), BF16)],
        compiler_params=pltpu.CompilerParams(
            dimension_semantics=("arbitrary",),
            vmem_limit_bytes=_vmem_limit(est)),
        name="conv_out",
    )(proj, proj, proj, proj, proj, conv_w, conv_b, ln_g, ln_b, w_pw2, b_pw2, y_att, w_out, x, g_post)


def _out_proj_kernel(ya_ref, yc_ref, w_ref, x_ref, g_ref, o_ref, wb_scr):
    @pl.when(pl.program_id(0) == 0)
    def _():
        wb_scr[...] = w_ref[...].astype(BF16)

    ca = ya_ref.shape[1]
    y = (jnp.dot(ya_ref[...], wb_scr[:ca, :], preferred_element_type=F32)
         + jnp.dot(yc_ref[...], wb_scr[ca:, :], preferred_element_type=F32))
    ms = jnp.mean(y * y, axis=-1, keepdims=True)
    o_ref[...] = x_ref[...] + y * lax.rsqrt(ms + RMS_EPS) * g_ref[...]


def _out_proj(y_att, y_conv, w_out, x, g_post, *, tm=512):
    s, d = x.shape
    ca, cc = y_att.shape[1], y_conv.shape[1]
    est = (ca + cc) * d * (4 + 2) + 2 * tm * (ca + cc) * 2 + 4 * tm * d * 4 + 2 * tm * d * 4
    return pl.pallas_call(
        _out_proj_kernel,
        out_shape=jax.ShapeDtypeStruct((s, d), x.dtype),
        grid=(s // tm,),
        in_specs=[
            pl.BlockSpec((tm, ca), lambda i: (i, 0)),
            pl.BlockSpec((tm, cc), lambda i: (i, 0)),
            pl.BlockSpec((ca + cc, d), lambda i: (0, 0), pipeline_mode=pl.Buffered(1)),
            pl.BlockSpec((tm, d), lambda i: (i, 0)),
            pl.BlockSpec((1, d), lambda i: (0, 0)),
        ],
        out_specs=pl.BlockSpec((tm, d), lambda i: (i, 0)),
        scratch_shapes=[pltpu.VMEM((ca + cc, d), BF16)],
        compiler_params=pltpu.CompilerParams(
            dimension_semantics=("arbitrary",),
            vmem_limit_bytes=_vmem_limit(est)),
        name="out_proj",
    )(y_att, y_conv, w_out, x, g_post)


def kernel(x, g_pre, w_in, conv_w, conv_b, ln_g, ln_b, w_pw2, b_pw2, w_out, g_post):
    b, s, d = x.shape
    assert b == 1, "rows of different batch entries must not share a causal history"
    c = conv_w.shape[-1]
    aw = w_out.shape[0] - c
    n_heads = aw // HEAD_DIM
    assert w_in.shape[1] == 4 * aw + 3 * c and aw == c

    col_scale = jnp.concatenate(
        [jnp.full((aw,), LOG2_E / math.sqrt(HEAD_DIM), F32), jnp.ones((w_in.shape[1] - aw,), F32)])

    x2 = x.reshape(s, d)
    proj = _in_proj(x2, g_pre.reshape(1, d), w_in, col_scale.reshape(1, -1))
    y_att = _sb_attn(proj, n_heads=n_heads)
    out = _conv_out(proj, y_att, conv_w.reshape(-1, V7X_LANES), conv_b.reshape(-1, V7X_LANES),
                    ln_g.reshape(1, c), ln_b.reshape(1, c), w_pw2, b_pw2.reshape(1, c),
                    w_out, x2, g_post.reshape(1, d), u_blk=(4 * aw) // c)
    return out.reshape(b, s, d)
```
